```python
import jax, jax.numpy as jnp
from jax import lax
import numpy as np

D_MODEL = 4096
BATCH = 4
SEQ = 2048
DEPTH = 4
DEC_BATCH = 32
DEC_SEQ = 1
PAST_LEN = 8192
PAGE_SIZE = 128

EPS = 1e-6
D_A = D_MODEL // 2
CONV_W = 31
N_HEADS = 32
N_KV = 4
HEAD_DIM = 64
GQ = N_HEADS // N_KV
D_B = N_HEADS * HEAD_DIM
KV_W = N_KV * HEAD_DIM
WINDOW = 128
ATT_BLOCK = 128
D_C = D_MODEL // 2
CHUNK = 128
SGU_GROUPS = 16
SGU_GROUP_DIM = D_C // SGU_GROUPS
IN_SIZES = (D_A, D_A, D_A, D_B, KV_W, KV_W, D_B, D_C, D_C, D_C, D_MODEL, D_MODEL, D_MODEL)
D_IN = sum(IN_SIZES)

kernel_name = "hybrid_conv_swa_sgu_decoder_step"


def _rmsnorm(x, g):
    xf = x.astype(jnp.float32)
    y = xf * lax.rsqrt(jnp.mean(xf * xf, axis=-1, keepdims=True) + EPS)
    return (y * g.astype(jnp.float32)).astype(x.dtype)


def _layernorm(x, g, b):
    xf = x.astype(jnp.float32)
    mu = jnp.mean(xf, axis=-1, keepdims=True)
    xc = xf - mu
    y = xc * lax.rsqrt(jnp.mean(xc * xc, axis=-1, keepdims=True) + EPS)
    return (y * g.astype(jnp.float32) + b.astype(jnp.float32)).astype(x.dtype)


def _split_in(h):
    idx = np.cumsum(IN_SIZES)[:-1].tolist()
    return jnp.split(h, idx, axis=-1)


def _depthwise_causal_conv(x_ext, w, b):
    c = x_ext.shape[-1]
    y = lax.conv_general_dilated(
        x_ext, w.astype(x_ext.dtype)[:, None, :], window_strides=(1,), padding="VALID",
        dimension_numbers=("NWC", "WIO", "NWC"), feature_group_count=c)
    return y + b.astype(y.dtype)


def _sink_attention(q, k, v, mask, sink):
    s = jnp.einsum("bnqkgd,bnskd->bnkgqs", q, k,
                   preferred_element_type=jnp.float32) * (HEAD_DIM ** -0.5)
    s = jnp.where(mask[None, :, None, None], s, -jnp.inf)
    sk = sink.astype(jnp.float32).reshape(N_KV, GQ)[None, None, :, :, None, None]
    m = jnp.maximum(jnp.max(s, axis=-1, keepdims=True), sk)
    p = jnp.exp(s - m)
    denom = jnp.sum(p, axis=-1, keepdims=True) + jnp.exp(sk - m)
    p = (p / denom).astype(v.dtype)
    return jnp.einsum("bnkgqs,bnskd->bnqkgd", p, v)


def _window_attention_prompt(q, k, v, sink):
    b, s = q.shape[:2]
    nb = s // ATT_BLOCK
    qb = q.reshape(b, nb, ATT_BLOCK, N_KV, GQ, HEAD_DIM)
    kb = k.reshape(b, nb, ATT_BLOCK, N_KV, HEAD_DIM)
    vb = v.reshape(b, nb, ATT_BLOCK, N_KV, HEAD_DIM)
    pad = ((0, 0), (1, 0), (0, 0), (0, 0), (0, 0))
    kk = jnp.concatenate([jnp.pad(kb, pad)[:, :-1], kb], axis=2)
    vv = jnp.concatenate([jnp.pad(vb, pad)[:, :-1], vb], axis=2)
    blk = jnp.arange(nb)[:, None, None] * ATT_BLOCK
    qpos = blk + jnp.arange(ATT_BLOCK)[None, :, None]
    kpos = blk - ATT_BLOCK + jnp.arange(2 * ATT_BLOCK)[None, None, :]
    d = qpos - kpos
    mask = (d >= 0) & (d <= WINDOW) & (kpos >= 0)
    o = _sink_attention(qb, kk, vv, mask, sink)
    return o.reshape(b, s, D_B)


def _window_attention_step(q, k_ext, v_ext, sink):
    b, t = q.shape[:2]
    l = k_ext.shape[1] - t
    d = jnp.arange(t)[:, None] + l - jnp.arange(l + t)[None, :]
    mask = ((d >= 0) & (d <= WINDOW))[None]
    o = _sink_attention(q[:, None], k_ext[:, None], v_ext[:, None], mask, sink)
    return o.reshape(b, t, D_B)


def _chunk_spatial_mix(v, w_s, b_s):
    b, t, _ = v.shape
    nc = -(-t // CHUNK)
    vp = jnp.pad(v, ((0, 0), (0, nc * CHUNK - t), (0, 0)))
    vp = vp.reshape(b, nc, CHUNK, SGU_GROUPS, SGU_GROUP_DIM)
    tril = jnp.tril(jnp.ones((CHUNK, CHUNK), dtype=bool))
    wm = jnp.where(tril[None], w_s, 0).astype(v.dtype)
    z = jnp.einsum("gts,bnsgc->bntgc", wm, vp) + b_s.T.astype(v.dtype)[None, None, :, :, None]
    return z.reshape(b, nc * CHUNK, D_C)[:, :t]


def _layer(x, w_in, conv_w, conv_b, conv_ln_g, conv_ln_b, attn_sink, sgu_ln_g, sgu_ln_b,
           sgu_w, sgu_b, w_a_out, w_b_out, w_c_out, w_o, g_pre, g_post,
           conv_buf=None, k_buf=None, v_buf=None):
    bn, t, _ = x.shape
    prompt = conv_buf is None
    xn = _rmsnorm(x, g_pre)
    (a_lin, a_glu, a_gate, q, k, v, b_gate, c_u, c_v, c_gate,
     m_a, m_b, m_c) = _split_in(xn @ w_in)

    glu = a_lin * jax.nn.sigmoid(a_glu)
    if prompt:
        conv_ext = jnp.pad(glu, ((0, 0), (CONV_W - 1, 0), (0, 0)))
    else:
        conv_ext = jnp.concatenate([conv_buf.astype(glu.dtype), glu], axis=1)
    conv_state = conv_ext[:, -(CONV_W - 1):]
    ya = jax.nn.silu(_layernorm(_depthwise_causal_conv(conv_ext, conv_w, conv_b),
                                conv_ln_g, conv_ln_b))
    ya = (ya * jax.nn.silu(a_gate)) @ w_a_out

    q = q.reshape(bn, t, N_KV, GQ, HEAD_DIM)
    k = k.reshape(bn, t, N_KV, HEAD_DIM)
    v = v.reshape(bn, t, N_KV, HEAD_DIM)
    if prompt:
        o = _window_attention_prompt(q, k, v, attn_sink)
        k_state, v_state = k[:, -WINDOW:], v[:, -WINDOW:]
    else:
        k_ext = jnp.concatenate([k_buf.astype(k.dtype), k], axis=1)
        v_ext = jnp.concatenate([v_buf.astype(v.dtype), v], axis=1)
        o = _window_attention_step(q, k_ext, v_ext, attn_sink)
        buf_len = k_buf.shape[1]
        k_state, v_state = k_ext[:, -buf_len:], v_ext[:, -buf_len:]
    yb = (o * jax.nn.silu(b_gate)) @ w_b_out

    vn = _layernorm(c_v, sgu_ln_g, sgu_ln_b)
    yc = (c_u * _chunk_spatial_mix(vn, sgu_w, sgu_b) * jax.nn.silu(c_gate)) @ w_c_out

    merged = jax.nn.sigmoid(m_a) * ya + jax.nn.sigmoid(m_b) * yb + jax.nn.sigmoid(m_c) * yc
    x = x + _rmsnorm(merged @ w_o, g_post)
    return x, conv_state, k_state, v_state, vn


def setup_inputs(seed: int = 0) -> dict:
    key = jax.random.key(seed)
    ks = jax.random.split(key, 21)
    f32 = jnp.float32

    def nrm(k, shape, scale):
        return jax.random.normal(k, shape, f32) * scale

    buf = min(WINDOW, PAST_LEN)
    return {
        "x_prompt": nrm(ks[0], (BATCH, SEQ, D_MODEL), 1.0),
        "x_sample": nrm(ks[1], (DEC_BATCH, DEC_SEQ, D_MODEL), 1.0),
        "state_conv": nrm(ks[2], (DEPTH, DEC_BATCH, CONV_W - 1, D_A), 0.5),
        "cache_win_k": nrm(ks[3], (DEPTH, DEC_BATCH, buf, N_KV, HEAD_DIM), 1.0),
        "cache_win_v": nrm(ks[4], (DEPTH, DEC_BATCH, buf, N_KV, HEAD_DIM), 1.0),
        "w_in": nrm(ks[5], (DEPTH, D_MODEL, D_IN), D_MODEL ** -0.5),
        "conv_w": nrm(ks[6], (DEPTH, CONV_W, D_A), CONV_W ** -0.5),
        "conv_b": nrm(ks[7], (DEPTH, D_A), 0.01),
        "conv_ln_g": 1.0 + nrm(ks[8], (DEPTH, D_A), 0.01),
        "conv_ln_b": nrm(ks[9], (DEPTH, D_A), 0.01),
        "attn_sink": nrm(ks[10], (DEPTH, N_HEADS), 0.5),
        "sgu_ln_g": 1.0 + nrm(ks[11], (DEPTH, D_C), 0.01),
        "sgu_ln_b": nrm(ks[12], (DEPTH, D_C), 0.01),
        "sgu_w": nrm(ks[13], (DEPTH, SGU_GROUPS, CHUNK, CHUNK), CHUNK ** -0.5),
        "sgu_b": 1.0 + nrm(ks[14], (DEPTH, SGU_GROUPS, CHUNK), 0.01),
        "w_a_out": nrm(ks[15], (DEPTH, D_A, D_MODEL), D_A ** -0.5),
        "w_b_out": nrm(ks[16], (DEPTH, D_B, D_MODEL), D_B ** -0.5),
        "w_c_out": nrm(ks[17], (DEPTH, D_C, D_MODEL), D_C ** -0.5),
        "w_o": nrm(ks[18], (DEPTH, D_MODEL, D_MODEL), D_MODEL ** -0.5),
        "g_pre": 1.0 + nrm(ks[19], (DEPTH, D_MODEL), 0.01),
        "g_post": 1.0 + nrm(ks[20], (DEPTH, D_MODEL), 0.01),
    }


def reference(x_prompt, x_sample, state_conv, cache_win_k, cache_win_v, w_in, conv_w, conv_b,
              conv_ln_g, conv_ln_b, attn_sink, sgu_ln_g, sgu_ln_b, sgu_w, sgu_b,
              w_a_out, w_b_out, w_c_out, w_o, g_pre, g_post):
    yp, ys = x_prompt, x_sample
    conv_p, kp_l, vp_l, conv_s, ks_l, vs_l, cv_s = [], [], [], [], [], [], []
    for l in range(DEPTH):
        lw = (w_in[l], conv_w[l], conv_b[l], conv_ln_g[l], conv_ln_b[l], attn_sink[l],
              sgu_ln_g[l], sgu_ln_b[l], sgu_w[l], sgu_b[l], w_a_out[l], w_b_out[l],
              w_c_out[l], w_o[l], g_pre[l], g_post[l])
        yp, c_p, k_p, v_p, _ = _layer(yp, *lw)
        ys, c_s, k_s, v_s, chv = _layer(ys, *lw, state_conv[l], cache_win_k[l], cache_win_v[l])
        conv_p.append(c_p); kp_l.append(k_p); vp_l.append(v_p)
        conv_s.append(c_s); ks_l.append(k_s); vs_l.append(v_s); cv_s.append(chv)
    return (yp, ys, jnp.stack(conv_p), jnp.stack(kp_l), jnp.stack(vp_l),
            jnp.stack(conv_s), jnp.stack(ks_l), jnp.stack(vs_l), jnp.stack(cv_s))
```

```python
import jax
import jax.numpy as jnp
from jax import lax
from jax.experimental import pallas as pl
from jax.experimental.pallas import tpu as pltpu

F32 = jnp.float32
BF16 = jnp.bfloat16

D_MODEL = 4096
BATCH = 4
SEQ = 2048
DEPTH = 4
DEC_BATCH = 32
EPS = 1e-6
D_A = D_MODEL // 2
CONV_W = 31
N_HEADS = 32
N_KV = 4
HEAD_DIM = 64
GQ = N_HEADS // N_KV
D_B = N_HEADS * HEAD_DIM
KV_W = N_KV * HEAD_DIM
WINDOW = 128
D_C = D_MODEL // 2
CHUNK = 128
SGU_GROUPS = 16
D_IN = 3 * D_A + 2 * D_B + 2 * KV_W + 3 * D_C + 3 * D_MODEL

LANES = 128
MIB = 1024 * 1024

M_PROMPT = BATCH * SEQ
M_SAMPLE_PAD = 128
M_ROWS = M_PROMPT + M_SAMPLE_PAD
BM = 1040
BN = 256
N_ROW_TILES = M_ROWS // BM

OFF_A_LIN = 0
OFF_A_GLU = D_A // BN
OFF_A_GATE = 2 * D_A // BN
OFF_B = 3 * D_A // BN
OFF_C_U = (3 * D_A + 2 * D_B + 2 * KV_W) // BN
OFF_C_V = OFF_C_U + D_C // BN
OFF_C_GATE = OFF_C_V + D_C // BN
OFF_M_A = OFF_C_GATE + D_C // BN
OFF_M_B = OFF_M_A + D_MODEL // BN
OFF_M_C = OFF_M_B + D_MODEL // BN

CONV_T = 128
CONV_HALO = 32
CONV_LC = 256


def _params(semantics, vmem_mib):
    return pltpu.CompilerParams(dimension_semantics=semantics,
                                vmem_limit_bytes=vmem_mib * MIB)


def _dot(a, b):
    return jnp.dot(a, b, preferred_element_type=F32)


def _sigmoid(x):
    return 1.0 / (1.0 + jnp.exp(-x))


def _silu(x):
    return x * _sigmoid(x)


def _cast_kernel(w_ref, o_ref):
    o_ref[...] = w_ref[...].astype(BF16)


def _to_bf16(w, rows, cols):
    depth, k, n = w.shape
    return pl.pallas_call(
        _cast_kernel,
        grid=(depth, k // rows, n // cols),
        in_specs=[pl.BlockSpec((None, rows, cols), lambda l, i, j: (l, i, j))],
        out_specs=pl.BlockSpec((None, rows, cols), lambda l, i, j: (l, i, j)),
        out_shape=jax.ShapeDtypeStruct(w.shape, BF16),
        compiler_params=_params(("arbitrary",) * 3, 48),
        name="cast_weights",
    )(w)


def _prenorm_kernel(x_ref, g_ref, o_ref):
    x = x_ref[...]
    ms = jnp.mean(x * x, axis=-1, keepdims=True)
    o_ref[...] = (x * lax.rsqrt(ms + EPS) * g_ref[...]).astype(BF16)


def _prenorm(x, g):
    rt = 520
    return pl.pallas_call(
        _prenorm_kernel,
        grid=(M_ROWS // rt,),
        in_specs=[pl.BlockSpec((rt, D_MODEL), lambda i: (i, 0)),
                  pl.BlockSpec((1, D_MODEL), lambda i: (0, 0))],
        out_specs=pl.BlockSpec((rt, D_MODEL), lambda i: (i, 0)),
        out_shape=jax.ShapeDtypeStruct((M_ROWS, D_MODEL), BF16),
        compiler_params=_params(("arbitrary",), 40),
        name="prenorm",
    )(x, g.reshape(1, D_MODEL))


def _w_in_spec(l, off):
    return pl.BlockSpec((None, D_MODEL, BN), lambda i, j: (l, 0, off + j))


def _xn_spec():
    return pl.BlockSpec((BM, D_MODEL), lambda i, j: (i, 0))


def _tile_spec():
    return pl.BlockSpec((BM, BN), lambda i, j: (i, j))


def _inproj_a_kernel(xn_ref, wl_ref, wg_ref, wt_ref, glu_ref, sa_ref):
    xn = xn_ref[...]
    lin = _dot(xn, wl_ref[...])
    gl = _dot(xn, wg_ref[...])
    glu_ref[...] = lin * _sigmoid(gl)
    gate = _dot(xn, wt_ref[...])
    sa_ref[...] = _silu(gate).astype(BF16)


def _inproj_a(xn, w_in, l):
    return pl.pallas_call(
        _inproj_a_kernel,
        grid=(N_ROW_TILES, D_A // BN),
        in_specs=[_xn_spec(), _w_in_spec(l, OFF_A_LIN), _w_in_spec(l, OFF_A_GLU),
                  _w_in_spec(l, OFF_A_GATE)],
        out_specs=[_tile_spec(), _tile_spec()],
        out_shape=[jax.ShapeDtypeStruct((M_ROWS, D_A), F32),
                   jax.ShapeDtypeStruct((M_ROWS, D_A), BF16)],
        compiler_params=_params(("arbitrary", "arbitrary"), 48),
        name="inproj_a",
    )(xn, w_in, w_in, w_in)


NB_Q = D_B // BN
NB_KV = 2 * KV_W // BN
NB_B = 2 * NB_Q + NB_KV


def _inproj_b_kernel(xn_ref, w_ref, q_ref, kvb_ref, kv_ref, sb_ref):
    j = pl.program_id(1)
    acc = _dot(xn_ref[...], w_ref[...])

    @pl.when(j < NB_Q)
    def _():
        q_ref[...] = (acc * (HEAD_DIM ** -0.5)).astype(BF16)

    @pl.when(jnp.logical_and(j >= NB_Q, j < NB_Q + NB_KV))
    def _():
        kv_ref[...] = acc
        kvb_ref[...] = acc.astype(BF16)

    @pl.when(j >= NB_Q + NB_KV)
    def _():
        sb_ref[...] = _silu(acc).astype(BF16)


def _inproj_b(xn, w_in, l):
    kv_map = lambda i, j: (i, jnp.clip(j - NB_Q, 0, NB_KV - 1))
    return pl.pallas_call(
        _inproj_b_kernel,
        grid=(N_ROW_TILES, NB_B),
        in_specs=[_xn_spec(), _w_in_spec(l, OFF_B)],
        out_specs=[pl.BlockSpec((BM, BN), lambda i, j: (i, jnp.minimum(j, NB_Q - 1))),
                   pl.BlockSpec((BM, BN), kv_map),
                   pl.BlockSpec((BM, BN), kv_map),
                   pl.BlockSpec((BM, BN),
                                lambda i, j: (i, jnp.clip(j - NB_Q - NB_KV, 0, NB_Q - 1)))],
        out_shape=[jax.ShapeDtypeStruct((M_ROWS, D_B), BF16),
                   jax.ShapeDtypeStruct((M_ROWS, 2 * KV_W), BF16),
                   jax.ShapeDtypeStruct((M_ROWS, 2 * KV_W), F32),
                   jax.ShapeDtypeStruct((M_ROWS, D_B), BF16)],
        compiler_params=_params(("arbitrary", "arbitrary"), 40),
        name="inproj_b",
    )(xn, w_in)


def _inproj_c_kernel(xn_ref, wu_ref, wv_ref, wt_ref, ug_ref, cv_ref):
    xn = xn_ref[...]
    u = _dot(xn, wu_ref[...])
    gate = _dot(xn, wt_ref[...])
    ug_ref[...] = (u * _silu(gate)).astype(BF16)
    cv_ref[...] = _dot(xn, wv_ref[...])


def _inproj_c(xn, w_in, l):
    return pl.pallas_call(
        _inproj_c_kernel,
        grid=(N_ROW_TILES, D_C // BN),
        in_specs=[_xn_spec(), _w_in_spec(l, OFF_C_U), _w_in_spec(l, OFF_C_V),
                  _w_in_spec(l, OFF_C_GATE)],
        out_specs=[_tile_spec(), _tile_spec()],
        out_shape=[jax.ShapeDtypeStruct((M_ROWS, D_C), BF16),
                   jax.ShapeDtypeStruct((M_ROWS, D_C), F32)],
        compiler_params=_params(("arbitrary", "arbitrary"), 48),
        name="inproj_c",
    )(xn, w_in, w_in, w_in)


def _ln_silu_gate(y, g, b, gate):
    mu = jnp.mean(y, axis=-1, keepdims=True)
    yc = y - mu
    var = jnp.mean(yc * yc, axis=-1, keepdims=True)
    yn = yc * lax.rsqrt(var + EPS) * g + b
    return (_silu(yn) * gate.astype(F32)).astype(BF16)


def _conv_prompt_kernel(cur_ref, prev_ref, sa_ref, w_ref, cb_ref, g_ref, b_ref, out_ref,
                        ext_ref, y_ref):
    t = pl.program_id(1)
    ext_ref[0:CONV_HALO, :] = jnp.where(t == 0, 0.0, prev_ref[...])
    ext_ref[CONV_HALO:CONV_HALO + CONV_T, :] = cur_ref[...]
    first = CONV_HALO - (CONV_W - 1)
    for c0 in range(0, D_A, CONV_LC):
        acc = jnp.broadcast_to(cb_ref[:, c0:c0 + CONV_LC], (CONV_T, CONV_LC))
        for j in range(CONV_W):
            acc = acc + (w_ref[j:j + 1, c0:c0 + CONV_LC]
                         * ext_ref[first + j:first + j + CONV_T, c0:c0 + CONV_LC])
        y_ref[:, c0:c0 + CONV_LC] = acc
    out_ref[...] = _ln_silu_gate(y_ref[...], g_ref[...], b_ref[...], sa_ref[...])


def _conv_prompt(glu, sa, conv_w, conv_b, ln_g, ln_b):
    nt = SEQ // CONV_T
    halo_per_tile = CONV_T // CONV_HALO

    def prev_map(b, t):
        return (jnp.maximum((b * nt + t) * halo_per_tile - 1, 0), 0)

    row = lambda b, t: (b * nt + t, 0)
    const = lambda b, t: (0, 0)
    return pl.pallas_call(
        _conv_prompt_kernel,
        grid=(BATCH, nt),
        in_specs=[pl.BlockSpec((CONV_T, D_A), row),
                  pl.BlockSpec((CONV_HALO, D_A), prev_map),
                  pl.BlockSpec((CONV_T, D_A), row),
                  pl.BlockSpec((CONV_W, D_A), const),
                  pl.BlockSpec((1, D_A), const),
                  pl.BlockSpec((1, D_A), const),
                  pl.BlockSpec((1, D_A), const)],
        out_specs=pl.BlockSpec((CONV_T, D_A), row),
        out_shape=jax.ShapeDtypeStruct((M_ROWS, D_A), BF16),
        scratch_shapes=[pltpu.VMEM((CONV_HALO + CONV_T, D_A), F32),
                        pltpu.VMEM((CONV_T, D_A), F32)],
        compiler_params=_params(("arbitrary", "arbitrary"), 32),
        name="conv_prompt",
    )(glu, glu, sa, conv_w, conv_b.reshape(1, D_A), ln_g.reshape(1, D_A),
      ln_b.reshape(1, D_A))


def _conv_sample_kernel(za_in_ref, st_ref, glu_ref, sa_ref, w_ref, cb_ref, g_ref, b_ref,
                        out_ref):
    del za_in_ref
    acc = cb_ref[...] + w_ref[CONV_W - 1:CONV_W, :] * glu_ref[0:DEC_BATCH, :]
    for j in range(CONV_W - 1):
        acc = acc + w_ref[j:j + 1, :] * st_ref[:, j, :]
    out_ref[0:DEC_BATCH, :] = _ln_silu_gate(acc, g_ref[...], b_ref[...],
                                            sa_ref[0:DEC_BATCH, :])
    out_ref[DEC_BATCH:, :] = jnp.zeros((M_SAMPLE_PAD - DEC_BATCH, D_A), BF16)


def _conv_sample(za, state, glu, sa, conv_w, conv_b, ln_g, ln_b):
    sample_tile = lambda i: (M_PROMPT // M_SAMPLE_PAD, 0)
    const = lambda i: (0, 0)
    return pl.pallas_call(
        _conv_sample_kernel,
        grid=(1,),
        in_specs=[pl.BlockSpec(memory_space=pl.ANY),
                  pl.BlockSpec((DEC_BATCH, CONV_W - 1, D_A), lambda i: (0, 0, 0)),
                  pl.BlockSpec((M_SAMPLE_PAD, D_A), sample_tile),
                  pl.BlockSpec((M_SAMPLE_PAD, D_A), sample_tile),
                  pl.BlockSpec((CONV_W, D_A), const),
                  pl.BlockSpec((1, D_A), const),
                  pl.BlockSpec((1, D_A), const),
                  pl.BlockSpec((1, D_A), const)],
        out_specs=pl.BlockSpec((M_SAMPLE_PAD, D_A), sample_tile),
        out_shape=jax.ShapeDtypeStruct((M_ROWS, D_A), BF16),
        input_output_aliases={0: 0},
        compiler_params=_params(("arbitrary",), 40),
        name="conv_sample",
    )(za, state, glu, sa, conv_w, conv_b.reshape(1, D_A), ln_g.reshape(1, D_A),
      ln_b.reshape(1, D_A))


def _softmax_sink(s, sink):
    m = jnp.maximum(jnp.max(s, axis=-1, keepdims=True), sink)
    p = jnp.exp(s - m)
    denom = jnp.sum(p, axis=-1, keepdims=True) + jnp.exp(sink - m)
    return p / denom


def _attn_prompt_kernel(sink_ref, q_ref, kp_ref, kc_ref, vp_ref, vc_ref, sb_ref, out_ref):
    n = pl.program_id(1)
    qi = lax.broadcasted_iota(jnp.int32, (CHUNK, 2 * CHUNK), 0)
    kc_i = lax.broadcasted_iota(jnp.int32, (CHUNK, 2 * CHUNK), 1)
    mask = jnp.logical_and(kc_i >= qi, kc_i <= qi + WINDOW)
    mask = jnp.logical_and(mask, jnp.logical_or(n > 0, kc_i >= CHUNK))
    for kv in range(N_KV):
        ks = slice(kv * HEAD_DIM, (kv + 1) * HEAD_DIM)
        k = jnp.concatenate([kp_ref[:, ks], kc_ref[:, ks]], axis=0)
        v = jnp.concatenate([vp_ref[:, ks], vc_ref[:, ks]], axis=0)
        for g in range(GQ):
            h = kv * GQ + g
            hs = slice(h * HEAD_DIM, (h + 1) * HEAD_DIM)
            s = lax.dot_general(q_ref[:, hs], k, (((1,), (1,)), ((), ())),
                                preferred_element_type=F32)
            s = jnp.where(mask, s, -jnp.inf)
            p = _softmax_sink(s, sink_ref[h]).astype(BF16)
            o = _dot(p, v)
            out_ref[:, hs] = (o * sb_ref[:, hs].astype(F32)).astype(BF16)


def _attn_prompt(sink, q, kvb, sb):
    nb = SEQ // CHUNK
    row = lambda b, n: (b * nb + n, 0)
    prev_k = lambda b, n: (b * nb + jnp.maximum(n - 1, 0), 0)
    prev_v = lambda b, n: (b * nb + jnp.maximum(n - 1, 0), 1)
    cur_v = lambda b, n: (b * nb + n, 1)
    return pl.pallas_call(
        _attn_prompt_kernel,
        grid=(BATCH, nb),
        in_specs=[pl.BlockSpec(memory_space=pltpu.SMEM),
                  pl.BlockSpec((CHUNK, D_B), row),
                  pl.BlockSpec((CHUNK, KV_W), prev_k),
                  pl.BlockSpec((CHUNK, KV_W), row),
                  pl.BlockSpec((CHUNK, KV_W), prev_v),
                  pl.BlockSpec((CHUNK, KV_W), cur_v),
                  pl.BlockSpec((CHUNK, D_B), row)],
        out_specs=pl.BlockSpec((CHUNK, D_B), row),
        out_shape=jax.ShapeDtypeStruct((M_ROWS, D_B), BF16),
        compiler_params=_params(("arbitrary", "arbitrary"), 32),
        name="attn_prompt",
    )(sink, q, kvb, kvb, kvb, kvb, sb)


def _attn_sample_kernel(sink_ref, q_ref, kc_ref, vc_ref, kn_ref, vn_ref, sb_ref, out_ref):
    gi = lax.broadcasted_iota(jnp.int32, (1, GQ, 1), 1)
    for kv in range(N_KV):
        ks = slice(kv * HEAD_DIM, (kv + 1) * HEAD_DIM)
        q = q_ref[kv]
        qf = q.astype(F32)
        kc = kc_ref[:, :, ks].astype(BF16)
        vc = vc_ref[:, :, ks].astype(BF16)
        kn = kn_ref[:, ks].astype(BF16).astype(F32)
        vn = vn_ref[:, ks].astype(BF16).astype(F32)
        sink = jnp.zeros((1, GQ, 1), F32)
        for g in range(GQ):
            sink = jnp.where(gi == g, sink_ref[kv * GQ + g], sink)
        s_c = lax.dot_general(q, kc, (((2,), (2,)), ((0,), (0,))),
                              preferred_element_type=F32)
        s_n = jnp.sum(qf * kn[:, None, :], axis=-1, keepdims=True)
        m = jnp.maximum(jnp.maximum(jnp.max(s_c, axis=-1, keepdims=True), s_n), sink)
        p_c = jnp.exp(s_c - m)
        p_n = jnp.exp(s_n - m)
        denom = jnp.sum(p_c, axis=-1, keepdims=True) + p_n + jnp.exp(sink - m)
        p_c = (p_c / denom).astype(BF16)
        p_n = (p_n / denom).astype(BF16).astype(F32)
        o = lax.dot_general(p_c, vc, (((2,), (1,)), ((0,), (0,))),
                            preferred_element_type=F32)
        o = o + p_n * vn[:, None, :]
        out_ref[kv] = (o * sb_ref[kv].astype(F32)).astype(BF16)


def _attn_sample(sink, q_s, k_cache, v_cache, k_new, v_new, sb_s):
    return pl.pallas_call(
        _attn_sample_kernel,
        in_specs=[pl.BlockSpec(memory_space=pltpu.SMEM)]
        + [pl.BlockSpec(memory_space=pltpu.VMEM)] * 6,
        out_specs=pl.BlockSpec(memory_space=pltpu.VMEM),
        out_shape=jax.ShapeDtypeStruct((N_KV, DEC_BATCH, GQ, HEAD_DIM), BF16),
        compiler_params=pltpu.CompilerParams(vmem_limit_bytes=40 * MIB),
        name="attn_sample",
    )(sink, q_s, k_cache, v_cache, k_new, v_new, sb_s)


N_CHUNKS = M_PROMPT // CHUNK


def _sgu_kernel(cv_ref, ug_ref, w_ref, bt_ref, g_ref, b_ref, out_ref, vn_ref):
    c = pl.program_id(0)
    is_sample = c == N_CHUNKS
    x = cv_ref[...]
    mu = jnp.mean(x, axis=-1, keepdims=True)
    xc = x - mu
    var = jnp.mean(xc * xc, axis=-1, keepdims=True)
    vn = xc * lax.rsqrt(var + EPS) * g_ref[...] + b_ref[...]

    @pl.when(is_sample)
    def _():
        vn_ref[...] = vn

    vnb = vn.astype(BF16)
    ri = lax.broadcasted_iota(jnp.int32, (CHUNK, CHUNK), 0)
    ci = lax.broadcasted_iota(jnp.int32, (CHUNK, CHUNK), 1)
    tril = ri >= ci
    diag = ri == ci
    for g in range(SGU_GROUPS):
        gs = slice(g * CHUNK, (g + 1) * CHUNK)
        w = w_ref[g]
        w_first = jnp.broadcast_to(w[0:1, 0:1], (CHUNK, CHUNK))
        wm = jnp.where(is_sample, jnp.where(diag, w_first, 0.0), jnp.where(tril, w, 0.0))
        bias = bt_ref[:, g:g + 1]
        bias = jnp.where(is_sample, jnp.broadcast_to(bias[0:1, :], (CHUNK, 1)), bias)
        z = _dot(wm.astype(BF16), vnb[:, gs]) + bias
        out_ref[:, gs] = (ug_ref[:, gs].astype(F32) * z).astype(BF16)


def _sgu(cv, ug, sgu_w, sgu_bt, ln_g, ln_b):
    row = lambda c: (c, 0)
    const = lambda c: (0, 0)
    return pl.pallas_call(
        _sgu_kernel,
        grid=(N_CHUNKS + 1,),
        in_specs=[pl.BlockSpec((CHUNK, D_C), row),
                  pl.BlockSpec((CHUNK, D_C), row),
                  pl.BlockSpec((SGU_GROUPS, CHUNK, CHUNK), lambda c: (0, 0, 0)),
                  pl.BlockSpec((CHUNK, SGU_GROUPS), const),
                  pl.BlockSpec((1, D_C), const),
                  pl.BlockSpec((1, D_C), const)],
        out_specs=[pl.BlockSpec((CHUNK, D_C), row),
                   pl.BlockSpec((CHUNK, D_C), const)],
        out_shape=[jax.ShapeDtypeStruct((M_ROWS, D_C), BF16),
                   jax.ShapeDtypeStruct((CHUNK, D_C), F32)],
        compiler_params=_params(("arbitrary",), 32),
        name="sgu",
    )(cv, ug, sgu_w, sgu_bt, ln_g.reshape(1, D_C), ln_b.reshape(1, D_C))


def _merge_kernel(xn_ref, za_ref, zb_ref, zc_ref, wma_ref, wmb_ref, wmc_ref,
                  wa_ref, wb_ref, wc_ref, out_ref):
    xn = xn_ref[...]
    acc = _sigmoid(_dot(xn, wma_ref[...])) * _dot(za_ref[...], wa_ref[...])
    acc = acc + _sigmoid(_dot(xn, wmb_ref[...])) * _dot(zb_ref[...], wb_ref[...])
    acc = acc + _sigmoid(_dot(xn, wmc_ref[...])) * _dot(zc_ref[...], wc_ref[...])
    out_ref[...] = acc.astype(BF16)


def _merge(xn, za, zb, zc, w_in, w_a, w_b, w_c, l):
    once = pl.Buffered(1)
    z_spec = pl.BlockSpec((BM, D_A), lambda i, j: (i, 0), pipeline_mode=once)
    w_out_spec = pl.BlockSpec((None, D_A, BN), lambda i, j: (l, 0, j))
    return pl.pallas_call(
        _merge_kernel,
        grid=(N_ROW_TILES, D_MODEL // BN),
        in_specs=[pl.BlockSpec((BM, D_MODEL), lambda i, j: (i, 0), pipeline_mode=once),
                  z_spec, z_spec, z_spec,
                  _w_in_spec(l, OFF_M_A), _w_in_spec(l, OFF_M_B), _w_in_spec(l, OFF_M_C),
                  w_out_spec, w_out_spec, w_out_spec],
        out_specs=_tile_spec(),
        out_shape=jax.ShapeDtypeStruct((M_ROWS, D_MODEL), BF16),
        compiler_params=_params(("arbitrary", "arbitrary"), 56),
        name="merge",
    )(xn, za, zb, zc, w_in, w_in, w_in, w_a, w_b, w_c)


OUT_BN = 512


def _outproj_kernel(m_ref, w_ref, y_ref):
    y_ref[...] = _dot(m_ref[...], w_ref[...])


def _outproj(merged, w_o, l):
    return pl.pallas_call(
        _outproj_kernel,
        grid=(N_ROW_TILES, D_MODEL // OUT_BN),
        in_specs=[pl.BlockSpec((BM, D_MODEL), lambda i, j: (i, 0)),
                  pl.BlockSpec((None, D_MODEL, OUT_BN), lambda i, j: (l, 0, j))],
        out_specs=pl.BlockSpec((BM, OUT_BN), lambda i, j: (i, j)),
        out_shape=jax.ShapeDtypeStruct((M_ROWS, D_MODEL), F32),
        compiler_params=_params(("arbitrary", "arbitrary"), 40),
        name="outproj",
    )(merged, w_o)


def _postnorm_kernel(x_ref, y_ref, gpost_ref, gpre_ref, xo_ref, xn_ref):
    y = y_ref[...]
    ms = jnp.mean(y * y, axis=-1, keepdims=True)
    x = x_ref[...] + y * lax.rsqrt(ms + EPS) * gpost_ref[...]
    xo_ref[...] = x
    ms2 = jnp.mean(x * x, axis=-1, keepdims=True)
    xn_ref[...] = (x * lax.rsqrt(ms2 + EPS) * gpre_ref[...]).astype(BF16)


def _postnorm(x, y, g_post, g_pre_next):
    rt = 208
    row = lambda i: (i, 0)
    const = lambda i: (0, 0)
    return pl.pallas_call(
        _postnorm_kernel,
        grid=(M_ROWS // rt,),
        in_specs=[pl.BlockSpec((rt, D_MODEL), row),
                  pl.BlockSpec((rt, D_MODEL), row),
                  pl.BlockSpec((1, D_MODEL), const),
                  pl.BlockSpec((1, D_MODEL), const)],
        out_specs=[pl.BlockSpec((rt, D_MODEL), row),
                   pl.BlockSpec((rt, D_MODEL), row)],
        out_shape=[jax.ShapeDtypeStruct((M_ROWS, D_MODEL), F32),
                   jax.ShapeDtypeStruct((M_ROWS, D_MODEL), BF16)],
        input_output_aliases={0: 0},
        compiler_params=_params(("arbitrary",), 48),
        name="postnorm",
    )(x, y, g_post.reshape(1, D_MODEL), g_pre_next.reshape(1, D_MODEL))


def _heads_major(rows):
    return rows.reshape(DEC_BATCH, N_KV, GQ, HEAD_DIM).transpose(1, 0, 2, 3)


def kernel(x_prompt, x_sample, state_conv, cache_win_k, cache_win_v, w_in, conv_w, conv_b,
           conv_ln_g, conv_ln_b, attn_sink, sgu_ln_g, sgu_ln_b, sgu_w, sgu_b,
           w_a_out, w_b_out, w_c_out, w_o, g_pre, g_post):
    w_in_b = _to_bf16(w_in, 512, D_IN // 6)
    w_a_b = _to_bf16(w_a_out, 512, D_MODEL)
    w_b_b = _to_bf16(w_b_out, 512, D_MODEL)
    w_c_b = _to_bf16(w_c_out, 512, D_MODEL)
    w_o_b = _to_bf16(w_o, 512, D_MODEL)

    x = jnp.concatenate([
        x_prompt.reshape(M_PROMPT, D_MODEL),
        x_sample.reshape(DEC_BATCH, D_MODEL),
        jnp.zeros((M_SAMPLE_PAD - DEC_BATCH, D_MODEL), F32)], axis=0)
    xn = _prenorm(x, g_pre[0])

    s_rows = slice(M_PROMPT, M_PROMPT + DEC_BATCH)
    conv_p, k_p, v_p, conv_s, k_s, v_s, cv_s = [], [], [], [], [], [], []
    for l in range(DEPTH):
        glu, sa = _inproj_a(xn, w_in_b, l)
        q, kvb, kv, sb = _inproj_b(xn, w_in_b, l)
        ug, cv = _inproj_c(xn, w_in_b, l)

        za = _conv_prompt(glu, sa, conv_w[l], conv_b[l], conv_ln_g[l], conv_ln_b[l])
        za = _conv_sample(za, state_conv[l], glu, sa, conv_w[l], conv_b[l],
                          conv_ln_g[l], conv_ln_b[l])

        zb = _attn_prompt(attn_sink[l], q, kvb, sb)
        zb_s = _attn_sample(
            attn_sink[l], _heads_major(q[s_rows]),
            cache_win_k[l].reshape(DEC_BATCH, WINDOW, KV_W),
            cache_win_v[l].reshape(DEC_BATCH, WINDOW, KV_W),
            kv[s_rows, :KV_W], kv[s_rows, KV_W:], _heads_major(sb[s_rows]))
        zb_s = zb_s.transpose(1, 0, 2, 3).reshape(DEC_BATCH, D_B)
        zb_tail = jnp.concatenate(
            [zb_s, jnp.zeros((M_SAMPLE_PAD - DEC_BATCH, D_B), BF16)], axis=0)
        zb = lax.dynamic_update_slice(zb, zb_tail, (M_PROMPT, 0))

        zc, vn_s = _sgu(cv, ug, sgu_w[l], sgu_b[l].T, sgu_ln_g[l], sgu_ln_b[l])

        merged = _merge(xn, za, zb, zc, w_in_b, w_a_b, w_b_b, w_c_b, l)
        y = _outproj(merged, w_o_b, l)
        x, xn = _postnorm(x, y, g_post[l], g_pre[(l + 1) % DEPTH])

        glu_p = glu[:M_PROMPT].reshape(BATCH, SEQ, D_A)
        conv_p.append(glu_p[:, SEQ - (CONV_W - 1):])
        kv_p = kv[:M_PROMPT].reshape(BATCH, SEQ, 2, N_KV, HEAD_DIM)
        k_p.append(kv_p[:, SEQ - WINDOW:, 0])
        v_p.append(kv_p[:, SEQ - WINDOW:, 1])
        conv_s.append(jnp.concatenate([state_conv[l][:, 1:], glu[s_rows][:, None]], axis=1))
        kv_new = kv[s_rows].reshape(DEC_BATCH, 1, 2, N_KV, HEAD_DIM)
        k_s.append(jnp.concatenate([cache_win_k[l][:, 1:], kv_new[:, :, 0]], axis=1))
        v_s.append(jnp.concatenate([cache_win_v[l][:, 1:], kv_new[:, :, 1]], axis=1))
        cv_s.append(vn_s[:DEC_BATCH][:, None])

    y_prompt = x[:M_PROMPT].reshape(BATCH, SEQ, D_MODEL)
    y_sample = x[s_rows].reshape(DEC_BATCH, 1, D_MODEL)
    return (y_prompt, y_sample, jnp.stack(conv_p), jnp.stack(k_p), jnp.stack(v_p),
            jnp.stack(conv_s), jnp.stack(k_s), jnp.stack(v_s), jnp.stack(cv_s))
```

```python
import jax
import jax.numpy as jnp
from jax import lax
from jax.experimental import pallas as pl
from jax.experimental.pallas import tpu as pltpu

F32 = jnp.float32
BF16 = jnp.bfloat16

D_MODEL = 4096
BATCH = 4
SEQ = 2048
DEPTH = 4
DEC_BATCH = 32
EPS = 1e-6
D_A = D_MODEL // 2
CONV_W = 31
N_HEADS = 32
N_KV = 4
HEAD_DIM = 64
GQ = N_HEADS // N_KV
D_B = N_HEADS * HEAD_DIM
KV_W = N_KV * HEAD_DIM
WINDOW = 128
D_C = D_MODEL // 2
CHUNK = 128
SGU_GROUPS = 16
D_IN = 3 * D_A + 2 * D_B + 2 * KV_W + 3 * D_C + 3 * D_MODEL

LANES = 128
MIB = 1024 * 1024

M_PROMPT = BATCH * SEQ
M_SAMPLE_PAD = 128
M_ROWS = M_PROMPT + M_SAMPLE_PAD
BM = 1040
BN = 256
N_ROW_TILES = M_ROWS // BM

OFF_A_LIN = 0
OFF_A_GLU = D_A // BN
OFF_A_GATE = 2 * D_A // BN
OFF_B = 3 * D_A // BN
OFF_C_U = (3 * D_A + 2 * D_B + 2 * KV_W) // BN
OFF_C_V = OFF_C_U + D_C // BN
OFF_C_GATE = OFF_C_V + D_C // BN
OFF_M_A = OFF_C_GATE + D_C // BN
OFF_M_B = OFF_M_A + D_MODEL // BN
OFF_M_C = OFF_M_B + D_MODEL // BN

CONV_T = 128
CONV_HALO = 32
CONV_LC = 128
SUBLANES = 8
CONV_SH_ROWS = CONV_HALO + CONV_T - SUBLANES


def _params(semantics, vmem_mib):
    return pltpu.CompilerParams(dimension_semantics=semantics,
                                vmem_limit_bytes=vmem_mib * MIB)


def _dot(a, b):
    return jnp.dot(a, b, preferred_element_type=F32)


def _sigmoid(x):
    return 1.0 / (1.0 + jnp.exp(-x))


def _silu(x):
    return x * _sigmoid(x)


def _cast_kernel(w_ref, o_ref):
    o_ref[...] = w_ref[...].astype(BF16)


def _to_bf16(w, rows, cols):
    depth, k, n = w.shape
    return pl.pallas_call(
        _cast_kernel,
        grid=(depth, k // rows, n // cols),
        in_specs=[pl.BlockSpec((None, rows, cols), lambda l, i, j: (l, i, j))],
        out_specs=pl.BlockSpec((None, rows, cols), lambda l, i, j: (l, i, j)),
        out_shape=jax.ShapeDtypeStruct(w.shape, BF16),
        compiler_params=_params(("arbitrary",) * 3, 48),
        name="cast_weights",
    )(w)


ROW_T = 128
N_PROMPT_TILES = M_PROMPT // ROW_T


def _rms_scale(x, g):
    ms = jnp.mean(x * x, axis=-1, keepdims=True)
    return x * lax.rsqrt(ms + EPS) * g


def _prompt_tile(i):
    return (jnp.minimum(i, N_PROMPT_TILES - 1), 0)


def _prenorm_kernel(xp_ref, xs_ref, g_ref, o_ref):
    i = pl.program_id(0)

    @pl.when(i < N_PROMPT_TILES)
    def _():
        o_ref[...] = _rms_scale(xp_ref[...], g_ref[...]).astype(BF16)

    @pl.when(i == N_PROMPT_TILES)
    def _():
        o_ref[...] = _rms_scale(xs_ref[...], g_ref[...]).astype(BF16)


def _prenorm(xp, xs, g):
    const = lambda i: (0, 0)
    return pl.pallas_call(
        _prenorm_kernel,
        grid=(N_PROMPT_TILES + 1,),
        in_specs=[pl.BlockSpec((ROW_T, D_MODEL), _prompt_tile),
                  pl.BlockSpec((ROW_T, D_MODEL), const),
                  pl.BlockSpec((1, D_MODEL), const)],
        out_specs=pl.BlockSpec((ROW_T, D_MODEL), lambda i: (i, 0)),
        out_shape=jax.ShapeDtypeStruct((M_ROWS, D_MODEL), BF16),
        compiler_params=_params(("arbitrary",), 32),
        name="prenorm",
    )(xp, xs, g.reshape(1, D_MODEL))


def _w_in_spec(l, off):
    return pl.BlockSpec((None, D_MODEL, BN), lambda i, j: (l, 0, off + j))


def _xn_spec():
    return pl.BlockSpec((BM, D_MODEL), lambda i, j: (i, 0))


def _tile_spec():
    return pl.BlockSpec((BM, BN), lambda i, j: (i, j))


def _inproj_a_kernel(xn_ref, wl_ref, wg_ref, wt_ref, glu_ref, sa_ref):
    xn = xn_ref[...]
    lin = _dot(xn, wl_ref[...])
    gl = _dot(xn, wg_ref[...])
    glu_ref[...] = lin * _sigmoid(gl)
    gate = _dot(xn, wt_ref[...])
    sa_ref[...] = _silu(gate).astype(BF16)


def _inproj_a(xn, w_in, l):
    return pl.pallas_call(
        _inproj_a_kernel,
        grid=(N_ROW_TILES, D_A // BN),
        in_specs=[_xn_spec(), _w_in_spec(l, OFF_A_LIN), _w_in_spec(l, OFF_A_GLU),
                  _w_in_spec(l, OFF_A_GATE)],
        out_specs=[_tile_spec(), _tile_spec()],
        out_shape=[jax.ShapeDtypeStruct((M_ROWS, D_A), F32),
                   jax.ShapeDtypeStruct((M_ROWS, D_A), BF16)],
        compiler_params=_params(("arbitrary", "arbitrary"), 48),
        name="inproj_a",
    )(xn, w_in, w_in, w_in)


NB_Q = D_B // BN
NB_KV = 2 * KV_W // BN
NB_B = 2 * NB_Q + NB_KV


def _inproj_b_kernel(xn_ref, w_ref, q_ref, kv_ref, sb_ref):
    j = pl.program_id(1)
    acc = _dot(xn_ref[...], w_ref[...])

    @pl.when(j < NB_Q)
    def _():
        q_ref[...] = (acc * (HEAD_DIM ** -0.5)).astype(BF16)

    @pl.when(jnp.logical_and(j >= NB_Q, j < NB_Q + NB_KV))
    def _():
        kv_ref[...] = acc

    @pl.when(j >= NB_Q + NB_KV)
    def _():
        sb_ref[...] = _silu(acc).astype(BF16)


def _inproj_b(xn, w_in, l):
    kv_map = lambda i, j: (i, jnp.clip(j - NB_Q, 0, NB_KV - 1))
    return pl.pallas_call(
        _inproj_b_kernel,
        grid=(N_ROW_TILES, NB_B),
        in_specs=[_xn_spec(), _w_in_spec(l, OFF_B)],
        out_specs=[pl.BlockSpec((BM, BN), lambda i, j: (i, jnp.minimum(j, NB_Q - 1))),
                   pl.BlockSpec((BM, BN), kv_map),
                   pl.BlockSpec((BM, BN),
                                lambda i, j: (i, jnp.clip(j - NB_Q - NB_KV, 0, NB_Q - 1)))],
        out_shape=[jax.ShapeDtypeStruct((M_ROWS, D_B), BF16),
                   jax.ShapeDtypeStruct((M_ROWS, 2 * KV_W), F32),
                   jax.ShapeDtypeStruct((M_ROWS, D_B), BF16)],
        compiler_params=_params(("arbitrary", "arbitrary"), 40),
        name="inproj_b",
    )(xn, w_in)


def _inproj_c_kernel(xn_ref, wu_ref, wv_ref, wt_ref, ug_ref, cv_ref):
    xn = xn_ref[...]
    u = _dot(xn, wu_ref[...])
    gate = _dot(xn, wt_ref[...])
    ug_ref[...] = (u * _silu(gate)).astype(BF16)
    cv_ref[...] = _dot(xn, wv_ref[...])


def _inproj_c(xn, w_in, l):
    return pl.pallas_call(
        _inproj_c_kernel,
        grid=(N_ROW_TILES, D_C // BN),
        in_specs=[_xn_spec(), _w_in_spec(l, OFF_C_U), _w_in_spec(l, OFF_C_V),
                  _w_in_spec(l, OFF_C_GATE)],
        out_specs=[_tile_spec(), _tile_spec()],
        out_shape=[jax.ShapeDtypeStruct((M_ROWS, D_C), BF16),
                   jax.ShapeDtypeStruct((M_ROWS, D_C), F32)],
        compiler_params=_params(("arbitrary", "arbitrary"), 48),
        name="inproj_c",
    )(xn, w_in, w_in, w_in)


def _ln_silu_gate(y, g, b, gate):
    mu = jnp.mean(y, axis=-1, keepdims=True)
    yc = y - mu
    var = jnp.mean(yc * yc, axis=-1, keepdims=True)
    yn = yc * lax.rsqrt(var + EPS) * g + b
    return (_silu(yn) * gate.astype(F32)).astype(BF16)


def _conv_prompt_kernel(cur_ref, prev_ref, sa_ref, w_ref, cb_ref, g_ref, b_ref, out_ref,
                        ext_ref, sh_ref, y_ref):
    t = pl.program_id(1)
    ext_ref[0:CONV_HALO, :] = jnp.where(t == 0, 0.0, prev_ref[...])
    ext_ref[CONV_HALO:CONV_HALO + CONV_T, :] = cur_ref[...]
    for s in range(1, SUBLANES):
        sh_ref[s - 1] = ext_ref[s:s + CONV_SH_ROWS, :]
    first = CONV_HALO - (CONV_W - 1)
    for c0 in range(0, D_A, CONV_LC):
        cs = slice(c0, c0 + CONV_LC)
        acc = jnp.broadcast_to(cb_ref[:, cs], (CONV_T, CONV_LC))
        for j in range(CONV_W):
            phase = (first + j) % SUBLANES
            base = first + j - phase
            if phase == 0:
                window = ext_ref[base:base + CONV_T, cs]
            else:
                window = sh_ref[phase - 1, base:base + CONV_T, cs]
            acc = acc + w_ref[j:j + 1, cs] * window
        y_ref[:, cs] = acc
    out_ref[...] = _ln_silu_gate(y_ref[...], g_ref[...], b_ref[...], sa_ref[...])


def _conv_prompt(glu, sa, conv_w, conv_b, ln_g, ln_b):
    nt = SEQ // CONV_T
    halo_per_tile = CONV_T // CONV_HALO

    def prev_map(b, t):
        return (jnp.maximum((b * nt + t) * halo_per_tile - 1, 0), 0)

    row = lambda b, t: (b * nt + t, 0)
    const = lambda b, t: (0, 0)
    return pl.pallas_call(
        _conv_prompt_kernel,
        grid=(BATCH, nt),
        in_specs=[pl.BlockSpec((CONV_T, D_A), row),
                  pl.BlockSpec((CONV_HALO, D_A), prev_map),
                  pl.BlockSpec((CONV_T, D_A), row),
                  pl.BlockSpec((CONV_W, D_A), const),
                  pl.BlockSpec((1, D_A), const),
                  pl.BlockSpec((1, D_A), const),
                  pl.BlockSpec((1, D_A), const)],
        out_specs=pl.BlockSpec((CONV_T, D_A), row),
        out_shape=jax.ShapeDtypeStruct((M_ROWS, D_A), BF16),
        scratch_shapes=[pltpu.VMEM((CONV_HALO + CONV_T, D_A), F32),
                        pltpu.VMEM((SUBLANES - 1, CONV_SH_ROWS, D_A), F32),
                        pltpu.VMEM((CONV_T, D_A), F32)],
        compiler_params=_params(("arbitrary", "arbitrary"), 40),
        name="conv_prompt",
    )(glu, glu, sa, conv_w, conv_b.reshape(1, D_A), ln_g.reshape(1, D_A),
      ln_b.reshape(1, D_A))


def _conv_sample_kernel(za_in_ref, st_ref, glu_ref, sa_ref, w_ref, cb_ref, g_ref, b_ref,
                        out_ref):
    del za_in_ref
    acc = cb_ref[...] + w_ref[CONV_W - 1:CONV_W, :] * glu_ref[0:DEC_BATCH, :]
    for j in range(CONV_W - 1):
        acc = acc + w_ref[j:j + 1, :] * st_ref[:, j, :]
    out_ref[0:DEC_BATCH, :] = _ln_silu_gate(acc, g_ref[...], b_ref[...],
                                            sa_ref[0:DEC_BATCH, :])
    out_ref[DEC_BATCH:, :] = jnp.zeros((M_SAMPLE_PAD - DEC_BATCH, D_A), BF16)


def _conv_sample(za, state, glu, sa, conv_w, conv_b, ln_g, ln_b):
    sample_tile = lambda i: (M_PROMPT // M_SAMPLE_PAD, 0)
    const = lambda i: (0, 0)
    return pl.pallas_call(
        _conv_sample_kernel,
        grid=(1,),
        in_specs=[pl.BlockSpec(memory_space=pl.ANY),
                  pl.BlockSpec((DEC_BATCH, CONV_W - 1, D_A), lambda i: (0, 0, 0)),
                  pl.BlockSpec((M_SAMPLE_PAD, D_A), sample_tile),
                  pl.BlockSpec((M_SAMPLE_PAD, D_A), sample_tile),
                  pl.BlockSpec((CONV_W, D_A), const),
                  pl.BlockSpec((1, D_A), const),
                  pl.BlockSpec((1, D_A), const),
                  pl.BlockSpec((1, D_A), const)],
        out_specs=pl.BlockSpec((M_SAMPLE_PAD, D_A), sample_tile),
        out_shape=jax.ShapeDtypeStruct((M_ROWS, D_A), BF16),
        input_output_aliases={0: 0},
        compiler_params=_params(("arbitrary",), 40),
        name="conv_sample",
    )(za, state, glu, sa, conv_w, conv_b.reshape(1, D_A), ln_g.reshape(1, D_A),
      ln_b.reshape(1, D_A))


def _softmax_sink_keys_major(s, sink):
    m = jnp.maximum(jnp.max(s, axis=0, keepdims=True), sink)
    p = jnp.exp(s - m)
    denom = jnp.sum(p, axis=0, keepdims=True) + jnp.exp(sink - m)
    return p / denom


def _attn_prompt_kernel(sink_ref, q_ref, kp_ref, kc_ref, vp_ref, vc_ref, sb_ref, out_ref,
                        s_ref, p_ref, vbd_ref):
    n = pl.program_id(1)
    nk = 2 * CHUNK
    kc_i = lax.broadcasted_iota(jnp.int32, (nk, CHUNK), 0)
    qi = lax.broadcasted_iota(jnp.int32, (nk, CHUNK), 1)
    mask = jnp.logical_and(kc_i >= qi, kc_i <= qi + WINDOW)
    mask = jnp.logical_and(mask, jnp.logical_or(n > 0, kc_i >= CHUNK))
    lane = lax.broadcasted_iota(jnp.int32, (nk, LANES), 1)
    lo = lane < HEAD_DIM
    zeros_t = jnp.zeros((HEAD_DIM, nk), F32)
    for slab in range(KV_W // LANES):
        ls = slice(slab * LANES, (slab + 1) * LANES)
        k_two = jnp.concatenate([kp_ref[:, ls], kc_ref[:, ls]], axis=0)
        v_two_t = jnp.concatenate([vp_ref[:, ls], vc_ref[:, ls]], axis=0).T
        k_sw = pltpu.roll(k_two, HEAD_DIM, axis=1)
        for half in range(LANES // HEAD_DIM):
            kv = slab * (LANES // HEAD_DIM) + half
            k_lo, k_hi = (k_two, k_sw) if half == 0 else (k_sw, k_two)
            kbd = jnp.concatenate([jnp.where(lo, k_lo, 0.0), jnp.where(lo, 0.0, k_hi)],
                                  axis=0).astype(BF16)
            vt = v_two_t[half * HEAD_DIM:(half + 1) * HEAD_DIM]
            vbd_ref[kv] = jnp.concatenate(
                [jnp.concatenate([vt, zeros_t], axis=1),
                 jnp.concatenate([zeros_t, vt], axis=1)], axis=0).astype(BF16)
            for pair in range(GQ // 2):
                h0 = kv * GQ + 2 * pair
                cs = slice(h0 * HEAD_DIM, h0 * HEAD_DIM + LANES)
                s_ref[h0 // 2] = lax.dot_general(kbd, q_ref[:, cs], (((1,), (1,)), ((), ())),
                                                 preferred_element_type=F32)
    for h in range(N_HEADS):
        rows = slice((h % 2) * nk, (h % 2 + 1) * nk)
        s = jnp.where(mask, s_ref[h // 2, rows, :], -jnp.inf)
        p_ref[h // 2, rows, :] = _softmax_sink_keys_major(s, sink_ref[h]).astype(BF16)
    for kv in range(N_KV):
        for pair in range(GQ // 2):
            h0 = kv * GQ + 2 * pair
            cs = slice(h0 * HEAD_DIM, h0 * HEAD_DIM + LANES)
            o = _dot(vbd_ref[kv], p_ref[h0 // 2]).T
            out_ref[:, cs] = (o * sb_ref[:, cs].astype(F32)).astype(BF16)


def _attn_prompt(sink, q, kvb, sb):
    nb = SEQ // CHUNK
    row = lambda b, n: (b * nb + n, 0)
    prev_k = lambda b, n: (b * nb + jnp.maximum(n - 1, 0), 0)
    prev_v = lambda b, n: (b * nb + jnp.maximum(n - 1, 0), 1)
    cur_v = lambda b, n: (b * nb + n, 1)
    return pl.pallas_call(
        _attn_prompt_kernel,
        grid=(BATCH, nb),
        in_specs=[pl.BlockSpec(memory_space=pltpu.SMEM),
                  pl.BlockSpec((CHUNK, D_B), row),
                  pl.BlockSpec((CHUNK, KV_W), prev_k),
                  pl.BlockSpec((CHUNK, KV_W), row),
                  pl.BlockSpec((CHUNK, KV_W), prev_v),
                  pl.BlockSpec((CHUNK, KV_W), cur_v),
                  pl.BlockSpec((CHUNK, D_B), row)],
        out_specs=pl.BlockSpec((CHUNK, D_B), row),
        out_shape=jax.ShapeDtypeStruct((M_ROWS, D_B), BF16),
        scratch_shapes=[pltpu.VMEM((N_HEADS // 2, 4 * CHUNK, CHUNK), F32),
                        pltpu.VMEM((N_HEADS // 2, 4 * CHUNK, CHUNK), BF16),
                        pltpu.VMEM((N_KV, LANES, 4 * CHUNK), BF16)],
        compiler_params=_params(("arbitrary", "arbitrary"), 32),
        name="attn_prompt",
    )(sink, q, kvb, kvb, kvb, kvb, sb)


def _attn_sample_kernel(sink_ref, q_ref, kc_ref, vc_ref, kn_ref, vn_ref, sb_ref, out_ref):
    gi = lax.broadcasted_iota(jnp.int32, (1, GQ, 1), 1)
    for kv in range(N_KV):
        q = q_ref[kv]
        qf = q.astype(F32)
        kc = kc_ref[:, :, kv, :].astype(BF16)
        vc = vc_ref[:, :, kv, :].astype(BF16)
        kn = kn_ref[kv].astype(BF16).astype(F32)
        vn = vn_ref[kv].astype(BF16).astype(F32)
        sink = jnp.zeros((1, GQ, 1), F32)
        for g in range(GQ):
            sink = jnp.where(gi == g, sink_ref[kv * GQ + g], sink)
        s_c = lax.dot_general(q, kc, (((2,), (2,)), ((0,), (0,))),
                              preferred_element_type=F32)
        s_n = jnp.sum(qf * kn[:, None, :], axis=-1, keepdims=True)
        m = jnp.maximum(jnp.maximum(jnp.max(s_c, axis=-1, keepdims=True), s_n), sink)
        p_c = jnp.exp(s_c - m)
        p_n = jnp.exp(s_n - m)
        denom = jnp.sum(p_c, axis=-1, keepdims=True) + p_n + jnp.exp(sink - m)
        p_c = (p_c / denom).astype(BF16)
        p_n = (p_n / denom).astype(BF16).astype(F32)
        o = lax.dot_general(p_c, vc, (((2,), (1,)), ((0,), (0,))),
                            preferred_element_type=F32)
        o = o + p_n * vn[:, None, :]
        out_ref[kv] = (o * sb_ref[kv].astype(F32)).astype(BF16)


def _attn_sample(sink, q_s, k_cache, v_cache, k_new, v_new, sb_s):
    return pl.pallas_call(
        _attn_sample_kernel,
        in_specs=[pl.BlockSpec(memory_space=pltpu.SMEM)]
        + [pl.BlockSpec(memory_space=pltpu.VMEM)] * 6,
        out_specs=pl.BlockSpec(memory_space=pltpu.VMEM),
        out_shape=jax.ShapeDtypeStruct((N_KV, DEC_BATCH, GQ, HEAD_DIM), BF16),
        compiler_params=pltpu.CompilerParams(vmem_limit_bytes=48 * MIB),
        name="attn_sample",
    )(sink, q_s, k_cache, v_cache, k_new, v_new, sb_s)


N_CHUNKS = M_PROMPT // CHUNK


def _sgu_kernel(cv_ref, ug_ref, w_ref, bt_ref, g_ref, b_ref, out_ref, vn_ref):
    c = pl.program_id(0)
    is_sample = c == N_CHUNKS
    x = cv_ref[...]
    mu = jnp.mean(x, axis=-1, keepdims=True)
    xc = x - mu
    var = jnp.mean(xc * xc, axis=-1, keepdims=True)
    vn = xc * lax.rsqrt(var + EPS) * g_ref[...] + b_ref[...]

    @pl.when(is_sample)
    def _():
        vn_ref[...] = vn

    vnb = vn.astype(BF16)
    ri = lax.broadcasted_iota(jnp.int32, (CHUNK, CHUNK), 0)
    ci = lax.broadcasted_iota(jnp.int32, (CHUNK, CHUNK), 1)
    tril = ri >= ci
    diag = ri == ci
    for g in range(SGU_GROUPS):
        gs = slice(g * CHUNK, (g + 1) * CHUNK)
        w = w_ref[g]
        w_first = jnp.broadcast_to(w[0:1, 0:1], (CHUNK, CHUNK))
        wm = jnp.where(is_sample, jnp.where(diag, w_first, 0.0), jnp.where(tril, w, 0.0))
        bias = bt_ref[:, g:g + 1]
        bias = jnp.where(is_sample, jnp.broadcast_to(bias[0:1, :], (CHUNK, 1)), bias)
        z = _dot(wm.astype(BF16), vnb[:, gs]) + bias
        out_ref[:, gs] = (ug_ref[:, gs].astype(F32) * z).astype(BF16)


def _sgu(cv, ug, sgu_w, sgu_bt, ln_g, ln_b):
    row = lambda c: (c, 0)
    const = lambda c: (0, 0)
    return pl.pallas_call(
        _sgu_kernel,
        grid=(N_CHUNKS + 1,),
        in_specs=[pl.BlockSpec((CHUNK, D_C), row),
                  pl.BlockSpec((CHUNK, D_C), row),
                  pl.BlockSpec((SGU_GROUPS, CHUNK, CHUNK), lambda c: (0, 0, 0)),
                  pl.BlockSpec((CHUNK, SGU_GROUPS), const),
                  pl.BlockSpec((1, D_C), const),
                  pl.BlockSpec((1, D_C), const)],
        out_specs=[pl.BlockSpec((CHUNK, D_C), row),
                   pl.BlockSpec((CHUNK, D_C), const)],
        out_shape=[jax.ShapeDtypeStruct((M_ROWS, D_C), BF16),
                   jax.ShapeDtypeStruct((CHUNK, D_C), F32)],
        compiler_params=_params(("arbitrary",), 32),
        name="sgu",
    )(cv, ug, sgu_w, sgu_bt, ln_g.reshape(1, D_C), ln_b.reshape(1, D_C))


def _merge_kernel(xn_ref, za_ref, zb_ref, zc_ref, wma_ref, wmb_ref, wmc_ref,
                  wa_ref, wb_ref, wc_ref, out_ref):
    xn = xn_ref[...]
    acc = _sigmoid(_dot(xn, wma_ref[...])) * _dot(za_ref[...], wa_ref[...])
    acc = acc + _sigmoid(_dot(xn, wmb_ref[...])) * _dot(zb_ref[...], wb_ref[...])
    acc = acc + _sigmoid(_dot(xn, wmc_ref[...])) * _dot(zc_ref[...], wc_ref[...])
    out_ref[...] = acc.astype(BF16)


def _merge(xn, za, zb, zc, w_in, w_a, w_b, w_c, l):
    once = pl.Buffered(1)
    z_spec = pl.BlockSpec((BM, D_A), lambda i, j: (i, 0), pipeline_mode=once)
    w_out_spec = pl.BlockSpec((None, D_A, BN), lambda i, j: (l, 0, j))
    return pl.pallas_call(
        _merge_kernel,
        grid=(N_ROW_TILES, D_MODEL // BN),
        in_specs=[pl.BlockSpec((BM, D_MODEL), lambda i, j: (i, 0), pipeline_mode=once),
                  z_spec, z_spec, z_spec,
                  _w_in_spec(l, OFF_M_A), _w_in_spec(l, OFF_M_B), _w_in_spec(l, OFF_M_C),
                  w_out_spec, w_out_spec, w_out_spec],
        out_specs=_tile_spec(),
        out_shape=jax.ShapeDtypeStruct((M_ROWS, D_MODEL), BF16),
        compiler_params=_params(("arbitrary", "arbitrary"), 56),
        name="merge",
    )(xn, za, zb, zc, w_in, w_in, w_in, w_a, w_b, w_c)


OUT_BN = 512


def _outproj_kernel(m_ref, w_ref, y_ref):
    y_ref[...] = _dot(m_ref[...], w_ref[...])


def _outproj(merged, w_o, l):
    return pl.pallas_call(
        _outproj_kernel,
        grid=(N_ROW_TILES, D_MODEL // OUT_BN),
        in_specs=[pl.BlockSpec((BM, D_MODEL), lambda i, j: (i, 0)),
                  pl.BlockSpec((None, D_MODEL, OUT_BN), lambda i, j: (l, 0, j))],
        out_specs=pl.BlockSpec((BM, OUT_BN), lambda i, j: (i, j)),
        out_shape=jax.ShapeDtypeStruct((M_ROWS, D_MODEL), F32),
        compiler_params=_params(("arbitrary", "arbitrary"), 40),
        name="outproj",
    )(merged, w_o)


def _postnorm_kernel(xp_ref, xs_ref, y_ref, gpost_ref, gpre_ref, xpo_ref, xso_ref, xn_ref):
    i = pl.program_id(0)

    def step(x_ref, xo_ref):
        x = x_ref[...] + _rms_scale(y_ref[...], gpost_ref[...])
        xo_ref[...] = x
        xn_ref[...] = _rms_scale(x, gpre_ref[...]).astype(BF16)

    @pl.when(i < N_PROMPT_TILES)
    def _():
        step(xp_ref, xpo_ref)

    @pl.when(i == N_PROMPT_TILES)
    def _():
        step(xs_ref, xso_ref)


def _postnorm(xp, xs, y, g_post, g_pre_next, in_place):
    row = lambda i: (i, 0)
    const = lambda i: (0, 0)
    return pl.pallas_call(
        _postnorm_kernel,
        grid=(N_PROMPT_TILES + 1,),
        in_specs=[pl.BlockSpec((ROW_T, D_MODEL), _prompt_tile),
                  pl.BlockSpec((ROW_T, D_MODEL), const),
                  pl.BlockSpec((ROW_T, D_MODEL), row),
                  pl.BlockSpec((1, D_MODEL), const),
                  pl.BlockSpec((1, D_MODEL), const)],
        out_specs=[pl.BlockSpec((ROW_T, D_MODEL), _prompt_tile),
                   pl.BlockSpec((ROW_T, D_MODEL), const),
                   pl.BlockSpec((ROW_T, D_MODEL), row)],
        out_shape=[jax.ShapeDtypeStruct((M_PROMPT, D_MODEL), F32),
                   jax.ShapeDtypeStruct((M_SAMPLE_PAD, D_MODEL), F32),
                   jax.ShapeDtypeStruct((M_ROWS, D_MODEL), BF16)],
        input_output_aliases={0: 0, 1: 1} if in_place else {},
        compiler_params=_params(("arbitrary",), 32),
        name="postnorm",
    )(xp, xs, y, g_post.reshape(1, D_MODEL), g_pre_next.reshape(1, D_MODEL))


def _heads_major(rows):
    return rows.reshape(DEC_BATCH, N_KV, GQ, HEAD_DIM).transpose(1, 0, 2, 3)


def kernel(x_prompt, x_sample, state_conv, cache_win_k, cache_win_v, w_in, conv_w, conv_b,
           conv_ln_g, conv_ln_b, attn_sink, sgu_ln_g, sgu_ln_b, sgu_w, sgu_b,
           w_a_out, w_b_out, w_c_out, w_o, g_pre, g_post):
    w_in_b = _to_bf16(w_in, 512, D_IN // 6)
    w_a_b = _to_bf16(w_a_out, 512, D_MODEL)
    w_b_b = _to_bf16(w_b_out, 512, D_MODEL)
    w_c_b = _to_bf16(w_c_out, 512, D_MODEL)
    w_o_b = _to_bf16(w_o, 512, D_MODEL)

    xp = x_prompt.reshape(M_PROMPT, D_MODEL)
    xs = jnp.pad(x_sample.reshape(DEC_BATCH, D_MODEL),
                 ((0, M_SAMPLE_PAD - DEC_BATCH), (0, 0)))
    xn = _prenorm(xp, xs, g_pre[0])

    s_rows = slice(M_PROMPT, M_PROMPT + DEC_BATCH)
    conv_p, k_p, v_p, conv_s, k_s, v_s, cv_s = [], [], [], [], [], [], []
    for l in range(DEPTH):
        glu, sa = _inproj_a(xn, w_in_b, l)
        q, kv, sb = _inproj_b(xn, w_in_b, l)
        ug, cv = _inproj_c(xn, w_in_b, l)

        za = _conv_prompt(glu, sa, conv_w[l], conv_b[l], conv_ln_g[l], conv_ln_b[l])
        za = _conv_sample(za, state_conv[l], glu, sa, conv_w[l], conv_b[l],
                          conv_ln_g[l], conv_ln_b[l])

        kv_new = kv[s_rows].reshape(DEC_BATCH, 2, N_KV, HEAD_DIM)
        zb = _attn_prompt(attn_sink[l], q, kv, sb)
        zb_s = _attn_sample(
            attn_sink[l], _heads_major(q[s_rows]), cache_win_k[l], cache_win_v[l],
            kv_new[:, 0].transpose(1, 0, 2), kv_new[:, 1].transpose(1, 0, 2),
            _heads_major(sb[s_rows]))
        zb_s = zb_s.transpose(1, 0, 2, 3).reshape(DEC_BATCH, D_B)
        zb_tail = jnp.pad(zb_s, ((0, M_SAMPLE_PAD - DEC_BATCH), (0, 0)))
        zb = lax.dynamic_update_slice(zb, zb_tail, (M_PROMPT, 0))

        zc, vn_s = _sgu(cv, ug, sgu_w[l], sgu_b[l].T, sgu_ln_g[l], sgu_ln_b[l])

        merged = _merge(xn, za, zb, zc, w_in_b, w_a_b, w_b_b, w_c_b, l)
        y = _outproj(merged, w_o_b, l)
        xp, xs, xn = _postnorm(xp, xs, y, g_post[l], g_pre[(l + 1) % DEPTH], in_place=l > 0)

        seq_end = [(b + 1) * SEQ for b in range(BATCH)]
        conv_p.append(jnp.stack([glu[e - (CONV_W - 1):e] for e in seq_end]))
        kv_p = jnp.stack([kv[e - WINDOW:e] for e in seq_end])
        kv_p = kv_p.reshape(BATCH, WINDOW, 2, N_KV, HEAD_DIM)
        k_p.append(kv_p[:, :, 0])
        v_p.append(kv_p[:, :, 1])
        conv_s.append(jnp.concatenate([state_conv[l][:, 1:], glu[s_rows][:, None]], axis=1))
        k_s.append(jnp.concatenate([cache_win_k[l][:, 1:], kv_new[:, None, 0]], axis=1))
        v_s.append(jnp.concatenate([cache_win_v[l][:, 1:], kv_new[:, None, 1]], axis=1))
        cv_s.append(vn_s[:DEC_BATCH][:, None])

    y_prompt = xp.reshape(BATCH, SEQ, D_MODEL)
    y_sample = xs[:DEC_BATCH].reshape(DEC_BATCH, 1, D_MODEL)
    return (y_prompt, y_sample, jnp.stack(conv_p), jnp.stack(k_p), jnp.stack(v_p),
            jnp.stack(conv_s), jnp.stack(k_s), jnp.stack(v_s), jnp.stack(cv_s))
```

```python
import jax
import jax.numpy as jnp
from jax import lax
from jax.experimental import pallas as pl
from jax.experimental.pallas import tpu as pltpu

F32 = jnp.float32
BF16 = jnp.bfloat16

D_MODEL = 4096
BATCH = 4
SEQ = 2048
DEPTH = 4
DEC_BATCH = 32
EPS = 1e-6
D_A = D_MODEL // 2
CONV_W = 31
N_HEADS = 32
N_KV = 4
HEAD_DIM = 64
GQ = N_HEADS // N_KV
D_B = N_HEADS * HEAD_DIM
KV_W = N_KV * HEAD_DIM
WINDOW = 128
D_C = D_MODEL // 2
CHUNK = 128
SGU_GROUPS = 16
D_IN = 3 * D_A + 2 * D_B + 2 * KV_W + 3 * D_C + 3 * D_MODEL

LANES = 128
MIB = 1024 * 1024

M_PROMPT = BATCH * SEQ
M_SAMPLE_PAD = 128
M_ROWS = M_PROMPT + M_SAMPLE_PAD
BM = 1040
BN = 256
N_ROW_TILES = M_ROWS // BM

OFF_A_LIN = 0
OFF_A_GLU = D_A // BN
OFF_A_GATE = 2 * D_A // BN
OFF_B = 3 * D_A // BN
OFF_C_U = (3 * D_A + 2 * D_B + 2 * KV_W) // BN
OFF_C_V = OFF_C_U + D_C // BN
OFF_C_GATE = OFF_C_V + D_C // BN
OFF_M_A = OFF_C_GATE + D_C // BN
OFF_M_B = OFF_M_A + D_MODEL // BN
OFF_M_C = OFF_M_B + D_MODEL // BN

CONV_T = 128
CONV_HALO = 32
CONV_LC = 128
SUBLANES = 8
CONV_SH_ROWS = CONV_HALO + CONV_T - SUBLANES


def _params(semantics, vmem_mib):
    return pltpu.CompilerParams(dimension_semantics=semantics,
                                vmem_limit_bytes=vmem_mib * MIB)


def _dot(a, b):
    return jnp.dot(a, b, preferred_element_type=F32)


def _wdot(a, w_ref):
    return _dot(a, w_ref[...].astype(BF16))


def _sigmoid(x):
    return 1.0 / (1.0 + jnp.exp(-x))


def _silu(x):
    return x * _sigmoid(x)


def _cast_kernel(w_ref, o_ref):
    o_ref[...] = w_ref[...].astype(BF16)


def _to_bf16(w, rows, cols, col_start=0):
    depth, k, n = w.shape
    first = col_start // cols
    return pl.pallas_call(
        _cast_kernel,
        grid=(depth, k // rows, (n - col_start) // cols),
        in_specs=[pl.BlockSpec((None, rows, cols), lambda l, i, j: (l, i, first + j))],
        out_specs=pl.BlockSpec((None, rows, cols), lambda l, i, j: (l, i, j)),
        out_shape=jax.ShapeDtypeStruct((depth, k, n - col_start), BF16),
        compiler_params=_params(("arbitrary",) * 3, 48),
        name="cast_weights",
    )(w)


ROW_T = 128
N_PROMPT_TILES = M_PROMPT // ROW_T


def _rms_scale(x, g):
    ms = jnp.mean(x * x, axis=-1, keepdims=True)
    return x * lax.rsqrt(ms + EPS) * g


def _prompt_tile(i):
    return (jnp.minimum(i, N_PROMPT_TILES - 1), 0)


def _prenorm_kernel(xp_ref, xs_ref, g_ref, o_ref):
    i = pl.program_id(0)

    @pl.when(i < N_PROMPT_TILES)
    def _():
        o_ref[...] = _rms_scale(xp_ref[...], g_ref[...]).astype(BF16)

    @pl.when(i == N_PROMPT_TILES)
    def _():
        o_ref[...] = _rms_scale(xs_ref[...], g_ref[...]).astype(BF16)


def _prenorm(xp, xs, g):
    const = lambda i: (0, 0)
    return pl.pallas_call(
        _prenorm_kernel,
        grid=(N_PROMPT_TILES + 1,),
        in_specs=[pl.BlockSpec((ROW_T, D_MODEL), _prompt_tile),
                  pl.BlockSpec((ROW_T, D_MODEL), const),
                  pl.BlockSpec((1, D_MODEL), const)],
        out_specs=pl.BlockSpec((ROW_T, D_MODEL), lambda i: (i, 0)),
        out_shape=jax.ShapeDtypeStruct((M_ROWS, D_MODEL), BF16),
        compiler_params=_params(("arbitrary",), 32),
        name="prenorm",
    )(xp, xs, g.reshape(1, D_MODEL))


def _w_in_spec(l, off):
    return pl.BlockSpec((None, D_MODEL, BN), lambda i, j: (l, 0, off + j))


def _xn_spec():
    return pl.BlockSpec((BM, D_MODEL), lambda i, j: (i, 0))


def _tile_spec():
    return pl.BlockSpec((BM, BN), lambda i, j: (i, j))


def _inproj_a_kernel(xn_ref, wl_ref, wg_ref, wt_ref, glu_ref, sa_ref):
    xn = xn_ref[...]
    lin = _wdot(xn, wl_ref)
    gl = _wdot(xn, wg_ref)
    glu_ref[...] = lin * _sigmoid(gl)
    gate = _wdot(xn, wt_ref)
    sa_ref[...] = _silu(gate).astype(BF16)


def _inproj_a(xn, w_in, l):
    return pl.pallas_call(
        _inproj_a_kernel,
        grid=(N_ROW_TILES, D_A // BN),
        in_specs=[_xn_spec(), _w_in_spec(l, OFF_A_LIN), _w_in_spec(l, OFF_A_GLU),
                  _w_in_spec(l, OFF_A_GATE)],
        out_specs=[_tile_spec(), _tile_spec()],
        out_shape=[jax.ShapeDtypeStruct((M_ROWS, D_A), F32),
                   jax.ShapeDtypeStruct((M_ROWS, D_A), BF16)],
        compiler_params=_params(("arbitrary", "arbitrary"), 56),
        name="inproj_a",
    )(xn, w_in, w_in, w_in)


BN_B = 2 * KV_W
NB_Q = D_B // BN_B
NB_KV = 2 * KV_W // BN_B
NB_B = 2 * NB_Q + NB_KV


def _inproj_b_kernel(xn_ref, w_ref, q_ref, kv_ref, sb_ref):
    j = pl.program_id(1)
    acc = _wdot(xn_ref[...], w_ref)

    @pl.when(j < NB_Q)
    def _():
        q_ref[...] = (acc * (HEAD_DIM ** -0.5)).astype(BF16)

    @pl.when(jnp.logical_and(j >= NB_Q, j < NB_Q + NB_KV))
    def _():
        kv_ref[...] = acc

    @pl.when(j >= NB_Q + NB_KV)
    def _():
        sb_ref[...] = _silu(acc).astype(BF16)


def _inproj_b(xn, w_in, l):
    kv_map = lambda i, j: (i, jnp.clip(j - NB_Q, 0, NB_KV - 1))
    return pl.pallas_call(
        _inproj_b_kernel,
        grid=(N_ROW_TILES, NB_B),
        in_specs=[_xn_spec(),
                  pl.BlockSpec((None, D_MODEL, BN_B),
                               lambda i, j: (l, 0, OFF_B * BN // BN_B + j))],
        out_specs=[pl.BlockSpec((BM, BN_B), lambda i, j: (i, jnp.minimum(j, NB_Q - 1))),
                   pl.BlockSpec((BM, BN_B), kv_map),
                   pl.BlockSpec((BM, BN_B),
                                lambda i, j: (i, jnp.clip(j - NB_Q - NB_KV, 0, NB_Q - 1)))],
        out_shape=[jax.ShapeDtypeStruct((M_ROWS, D_B), BF16),
                   jax.ShapeDtypeStruct((M_ROWS, 2 * KV_W), F32),
                   jax.ShapeDtypeStruct((M_ROWS, D_B), BF16)],
        compiler_params=_params(("arbitrary", "arbitrary"), 56),
        name="inproj_b",
    )(xn, w_in)


def _inproj_c_kernel(xn_ref, wu_ref, wv_ref, wt_ref, ug_ref, cv_ref):
    xn = xn_ref[...]
    u = _wdot(xn, wu_ref)
    gate = _wdot(xn, wt_ref)
    ug_ref[...] = (u * _silu(gate)).astype(BF16)
    cv_ref[...] = _wdot(xn, wv_ref)


def _inproj_c(xn, w_in, l):
    return pl.pallas_call(
        _inproj_c_kernel,
        grid=(N_ROW_TILES, D_C // BN),
        in_specs=[_xn_spec(), _w_in_spec(l, OFF_C_U), _w_in_spec(l, OFF_C_V),
                  _w_in_spec(l, OFF_C_GATE)],
        out_specs=[_tile_spec(), _tile_spec()],
        out_shape=[jax.ShapeDtypeStruct((M_ROWS, D_C), BF16),
                   jax.ShapeDtypeStruct((M_ROWS, D_C), F32)],
        compiler_params=_params(("arbitrary", "arbitrary"), 56),
        name="inproj_c",
    )(xn, w_in, w_in, w_in)


def _ln_silu_gate(y, g, b, gate):
    mu = jnp.mean(y, axis=-1, keepdims=True)
    yc = y - mu
    var = jnp.mean(yc * yc, axis=-1, keepdims=True)
    yn = yc * lax.rsqrt(var + EPS) * g + b
    return (_silu(yn) * gate.astype(F32)).astype(BF16)


def _conv_prompt_kernel(cur_ref, prev_ref, sa_ref, w_ref, cb_ref, g_ref, b_ref, out_ref,
                        ext_ref, sh_ref, y_ref):
    t = pl.program_id(1)
    ext_ref[0:CONV_HALO, :] = jnp.where(t == 0, 0.0, prev_ref[...])
    ext_ref[CONV_HALO:CONV_HALO + CONV_T, :] = cur_ref[...]
    for s in range(1, SUBLANES):
        sh_ref[s - 1] = ext_ref[s:s + CONV_SH_ROWS, :]
    first = CONV_HALO - (CONV_W - 1)
    groups = range(0, CONV_T, SUBLANES)
    for c0 in range(0, D_A, CONV_LC):
        cs = slice(c0, c0 + CONV_LC)
        bias = jnp.broadcast_to(cb_ref[:, cs], (SUBLANES, CONV_LC))
        acc = [bias for _ in groups]
        for j in range(CONV_W):
            phase = (first + j) % SUBLANES
            base = first + j - phase
            wj = jnp.broadcast_to(w_ref[j:j + 1, cs], (SUBLANES, CONV_LC))
            for gi, r0 in enumerate(groups):
                rows = slice(base + r0, base + r0 + SUBLANES)
                window = ext_ref[rows, cs] if phase == 0 else sh_ref[phase - 1, rows, cs]
                acc[gi] = acc[gi] + wj * window
        for gi, r0 in enumerate(groups):
            y_ref[r0:r0 + SUBLANES, cs] = acc[gi]
    out_ref[...] = _ln_silu_gate(y_ref[...], g_ref[...], b_ref[...], sa_ref[...])


def _conv_prompt(glu, sa, conv_w, conv_b, ln_g, ln_b):
    nt = SEQ // CONV_T
    halo_per_tile = CONV_T // CONV_HALO

    def prev_map(b, t):
        return (jnp.maximum((b * nt + t) * halo_per_tile - 1, 0), 0)

    row = lambda b, t: (b * nt + t, 0)
    const = lambda b, t: (0, 0)
    return pl.pallas_call(
        _conv_prompt_kernel,
        grid=(BATCH, nt),
        in_specs=[pl.BlockSpec((CONV_T, D_A), row),
                  pl.BlockSpec((CONV_HALO, D_A), prev_map),
                  pl.BlockSpec((CONV_T, D_A), row),
                  pl.BlockSpec((CONV_W, D_A), const),
                  pl.BlockSpec((1, D_A), const),
                  pl.BlockSpec((1, D_A), const),
                  pl.BlockSpec((1, D_A), const)],
        out_specs=pl.BlockSpec((CONV_T, D_A), row),
        out_shape=jax.ShapeDtypeStruct((M_ROWS, D_A), BF16),
        scratch_shapes=[pltpu.VMEM((CONV_HALO + CONV_T, D_A), F32),
                        pltpu.VMEM((SUBLANES - 1, CONV_SH_ROWS, D_A), F32),
                        pltpu.VMEM((CONV_T, D_A), F32)],
        compiler_params=_params(("arbitrary", "arbitrary"), 40),
        name="conv_prompt",
    )(glu, glu, sa, conv_w, conv_b.reshape(1, D_A), ln_g.reshape(1, D_A),
      ln_b.reshape(1, D_A))


def _conv_sample_kernel(za_in_ref, st_ref, glu_ref, sa_ref, w_ref, cb_ref, g_ref, b_ref,
                        out_ref):
    del za_in_ref
    acc = cb_ref[...] + w_ref[CONV_W - 1:CONV_W, :] * glu_ref[0:DEC_BATCH, :]
    for j in range(CONV_W - 1):
        acc = acc + w_ref[j:j + 1, :] * st_ref[:, j, :]
    out_ref[0:DEC_BATCH, :] = _ln_silu_gate(acc, g_ref[...], b_ref[...],
                                            sa_ref[0:DEC_BATCH, :])
    out_ref[DEC_BATCH:, :] = jnp.zeros((M_SAMPLE_PAD - DEC_BATCH, D_A), BF16)


def _conv_sample(za, state, glu, sa, conv_w, conv_b, ln_g, ln_b):
    sample_tile = lambda i: (M_PROMPT // M_SAMPLE_PAD, 0)
    const = lambda i: (0, 0)
    return pl.pallas_call(
        _conv_sample_kernel,
        grid=(1,),
        in_specs=[pl.BlockSpec(memory_space=pl.ANY),
                  pl.BlockSpec((DEC_BATCH, CONV_W - 1, D_A), lambda i: (0, 0, 0)),
                  pl.BlockSpec((M_SAMPLE_PAD, D_A), sample_tile),
                  pl.BlockSpec((M_SAMPLE_PAD, D_A), sample_tile),
                  pl.BlockSpec((CONV_W, D_A), const),
                  pl.BlockSpec((1, D_A), const),
                  pl.BlockSpec((1, D_A), const),
                  pl.BlockSpec((1, D_A), const)],
        out_specs=pl.BlockSpec((M_SAMPLE_PAD, D_A), sample_tile),
        out_shape=jax.ShapeDtypeStruct((M_ROWS, D_A), BF16),
        input_output_aliases={0: 0},
        compiler_params=_params(("arbitrary",), 40),
        name="conv_sample",
    )(za, state, glu, sa, conv_w, conv_b.reshape(1, D_A), ln_g.reshape(1, D_A),
      ln_b.reshape(1, D_A))


def _softmax_sink_keys_major(s, sink):
    m = jnp.maximum(jnp.max(s, axis=0, keepdims=True), sink)
    p = jnp.exp(s - m)
    denom = jnp.sum(p, axis=0, keepdims=True) + jnp.exp(sink - m)
    return p / denom


def _attn_prompt_kernel(sink_ref, q_ref, kp_ref, kc_ref, vp_ref, vc_ref, sb_ref, out_ref,
                        s_ref, p_ref, vbd_ref):
    n = pl.program_id(1)
    nk = 2 * CHUNK
    kc_i = lax.broadcasted_iota(jnp.int32, (nk, CHUNK), 0)
    qi = lax.broadcasted_iota(jnp.int32, (nk, CHUNK), 1)
    mask = jnp.logical_and(kc_i >= qi, kc_i <= qi + WINDOW)
    mask = jnp.logical_and(mask, jnp.logical_or(n > 0, kc_i >= CHUNK))
    lane = lax.broadcasted_iota(jnp.int32, (nk, LANES), 1)
    lo = lane < HEAD_DIM
    zeros_t = jnp.zeros((HEAD_DIM, nk), F32)
    for slab in range(KV_W // LANES):
        ls = slice(slab * LANES, (slab + 1) * LANES)
        k_two = jnp.concatenate([kp_ref[:, ls], kc_ref[:, ls]], axis=0)
        v_two_t = jnp.concatenate([vp_ref[:, ls], vc_ref[:, ls]], axis=0).T
        k_sw = pltpu.roll(k_two, HEAD_DIM, axis=1)
        for half in range(LANES // HEAD_DIM):
            kv = slab * (LANES // HEAD_DIM) + half
            k_lo, k_hi = (k_two, k_sw) if half == 0 else (k_sw, k_two)
            kbd = jnp.concatenate([jnp.where(lo, k_lo, 0.0), jnp.where(lo, 0.0, k_hi)],
                                  axis=0).astype(BF16)
            vt = v_two_t[half * HEAD_DIM:(half + 1) * HEAD_DIM]
            vbd_ref[kv] = jnp.concatenate(
                [jnp.concatenate([vt, zeros_t], axis=1),
                 jnp.concatenate([zeros_t, vt], axis=1)], axis=0).astype(BF16)
            for pair in range(GQ // 2):
                h0 = kv * GQ + 2 * pair
                cs = slice(h0 * HEAD_DIM, h0 * HEAD_DIM + LANES)
                s_ref[h0 // 2] = lax.dot_general(kbd, q_ref[:, cs], (((1,), (1,)), ((), ())),
                                                 preferred_element_type=F32)
    for h in range(N_HEADS):
        rows = slice((h % 2) * nk, (h % 2 + 1) * nk)
        s = jnp.where(mask, s_ref[h // 2, rows, :], -jnp.inf)
        p_ref[h // 2, rows, :] = _softmax_sink_keys_major(s, sink_ref[h]).astype(BF16)
    for kv in range(N_KV):
        for pair in range(GQ // 2):
            h0 = kv * GQ + 2 * pair
            cs = slice(h0 * HEAD_DIM, h0 * HEAD_DIM + LANES)
            o = _dot(vbd_ref[kv], p_ref[h0 // 2]).T
            out_ref[:, cs] = (o * sb_ref[:, cs].astype(F32)).astype(BF16)


def _attn_prompt(sink, q, kvb, sb):
    nb = SEQ // CHUNK
    row = lambda b, n: (b * nb + n, 0)
    prev_k = lambda b, n: (b * nb + jnp.maximum(n - 1, 0), 0)
    prev_v = lambda b, n: (b * nb + jnp.maximum(n - 1, 0), 1)
    cur_v = lambda b, n: (b * nb + n, 1)
    return pl.pallas_call(
        _attn_prompt_kernel,
        grid=(BATCH, nb),
        in_specs=[pl.BlockSpec(memory_space=pltpu.SMEM),
                  pl.BlockSpec((CHUNK, D_B), row),
                  pl.BlockSpec((CHUNK, KV_W), prev_k),
                  pl.BlockSpec((CHUNK, KV_W), row),
                  pl.BlockSpec((CHUNK, KV_W), prev_v),
                  pl.BlockSpec((CHUNK, KV_W), cur_v),
                  pl.BlockSpec((CHUNK, D_B), row)],
        out_specs=pl.BlockSpec((CHUNK, D_B), row),
        out_shape=jax.ShapeDtypeStruct((M_ROWS, D_B), BF16),
        scratch_shapes=[pltpu.VMEM((N_HEADS // 2, 4 * CHUNK, CHUNK), F32),
                        pltpu.VMEM((N_HEADS // 2, 4 * CHUNK, CHUNK), BF16),
                        pltpu.VMEM((N_KV, LANES, 4 * CHUNK), BF16)],
        compiler_params=_params(("arbitrary", "arbitrary"), 32),
        name="attn_prompt",
    )(sink, q, kvb, kvb, kvb, kvb, sb)


def _attn_sample_kernel(sink_ref, q_ref, kc_ref, vc_ref, kn_ref, vn_ref, sb_ref, out_ref):
    gi = lax.broadcasted_iota(jnp.int32, (1, GQ, 1), 1)
    for kv in range(N_KV):
        q = q_ref[kv]
        qf = q.astype(F32)
        kc = kc_ref[:, :, kv, :].astype(BF16)
        vc = vc_ref[:, :, kv, :].astype(BF16)
        kn = kn_ref[kv].astype(BF16).astype(F32)
        vn = vn_ref[kv].astype(BF16).astype(F32)
        sink = jnp.zeros((1, GQ, 1), F32)
        for g in range(GQ):
            sink = jnp.where(gi == g, sink_ref[kv * GQ + g], sink)
        s_c = lax.dot_general(q, kc, (((2,), (2,)), ((0,), (0,))),
                              preferred_element_type=F32)
        s_n = jnp.sum(qf * kn[:, None, :], axis=-1, keepdims=True)
        m = jnp.maximum(jnp.maximum(jnp.max(s_c, axis=-1, keepdims=True), s_n), sink)
        p_c = jnp.exp(s_c - m)
        p_n = jnp.exp(s_n - m)
        denom = jnp.sum(p_c, axis=-1, keepdims=True) + p_n + jnp.exp(sink - m)
        p_c = (p_c / denom).astype(BF16)
        p_n = (p_n / denom).astype(BF16).astype(F32)
        o = lax.dot_general(p_c, vc, (((2,), (1,)), ((0,), (0,))),
                            preferred_element_type=F32)
        o = o + p_n * vn[:, None, :]
        out_ref[kv] = (o * sb_ref[kv].astype(F32)).astype(BF16)


def _attn_sample(sink, q_s, k_cache, v_cache, k_new, v_new, sb_s):
    return pl.pallas_call(
        _attn_sample_kernel,
        in_specs=[pl.BlockSpec(memory_space=pltpu.SMEM)]
        + [pl.BlockSpec(memory_space=pltpu.VMEM)] * 6,
        out_specs=pl.BlockSpec(memory_space=pltpu.VMEM),
        out_shape=jax.ShapeDtypeStruct((N_KV, DEC_BATCH, GQ, HEAD_DIM), BF16),
        compiler_params=pltpu.CompilerParams(vmem_limit_bytes=48 * MIB),
        name="attn_sample",
    )(sink, q_s, k_cache, v_cache, k_new, v_new, sb_s)


N_CHUNKS = M_PROMPT // CHUNK


def _sgu_kernel(cv_ref, ug_ref, w_ref, bt_ref, g_ref, b_ref, out_ref, vn_ref):
    c = pl.program_id(0)
    is_sample = c == N_CHUNKS
    x = cv_ref[...]
    mu = jnp.mean(x, axis=-1, keepdims=True)
    xc = x - mu
    var = jnp.mean(xc * xc, axis=-1, keepdims=True)
    vn = xc * lax.rsqrt(var + EPS) * g_ref[...] + b_ref[...]

    @pl.when(is_sample)
    def _():
        vn_ref[...] = vn

    vnb = vn.astype(BF16)
    ri = lax.broadcasted_iota(jnp.int32, (CHUNK, CHUNK), 0)
    ci = lax.broadcasted_iota(jnp.int32, (CHUNK, CHUNK), 1)
    tril = ri >= ci
    diag = ri == ci
    for g in range(SGU_GROUPS):
        gs = slice(g * CHUNK, (g + 1) * CHUNK)
        w = w_ref[g]
        w_first = jnp.broadcast_to(w[0:1, 0:1], (CHUNK, CHUNK))
        wm = jnp.where(is_sample, jnp.where(diag, w_first, 0.0), jnp.where(tril, w, 0.0))
        bias = bt_ref[:, g:g + 1]
        bias = jnp.where(is_sample, jnp.broadcast_to(bias[0:1, :], (CHUNK, 1)), bias)
        z = _dot(wm.astype(BF16), vnb[:, gs]) + bias
        out_ref[:, gs] = (ug_ref[:, gs].astype(F32) * z).astype(BF16)


def _sgu(cv, ug, sgu_w, sgu_bt, ln_g, ln_b):
    row = lambda c: (c, 0)
    const = lambda c: (0, 0)
    return pl.pallas_call(
        _sgu_kernel,
        grid=(N_CHUNKS + 1,),
        in_specs=[pl.BlockSpec((CHUNK, D_C), row),
                  pl.BlockSpec((CHUNK, D_C), row),
                  pl.BlockSpec((SGU_GROUPS, CHUNK, CHUNK), lambda c: (0, 0, 0)),
                  pl.BlockSpec((CHUNK, SGU_GROUPS), const),
                  pl.BlockSpec((1, D_C), const),
                  pl.BlockSpec((1, D_C), const)],
        out_specs=[pl.BlockSpec((CHUNK, D_C), row),
                   pl.BlockSpec((CHUNK, D_C), const)],
        out_shape=[jax.ShapeDtypeStruct((M_ROWS, D_C), BF16),
                   jax.ShapeDtypeStruct((CHUNK, D_C), F32)],
        compiler_params=_params(("arbitrary",), 32),
        name="sgu",
    )(cv, ug, sgu_w, sgu_bt, ln_g.reshape(1, D_C), ln_b.reshape(1, D_C))


def _merge_kernel(xn_ref, za_ref, zb_ref, zc_ref, wma_ref, wmb_ref, wmc_ref,
                  wa_ref, wb_ref, wc_ref, out_ref):
    xn = xn_ref[...]
    acc = _sigmoid(_dot(xn, wma_ref[...])) * _dot(za_ref[...], wa_ref[...])
    acc = acc + _sigmoid(_dot(xn, wmb_ref[...])) * _dot(zb_ref[...], wb_ref[...])
    acc = acc + _sigmoid(_dot(xn, wmc_ref[...])) * _dot(zc_ref[...], wc_ref[...])
    out_ref[...] = acc.astype(BF16)


def _merge(xn, za, zb, zc, w_gates, w_a, w_b, w_c, l):
    gate_spec = lambda g: _w_in_spec(l, g * (D_MODEL // BN))
    once = pl.Buffered(1)
    z_spec = pl.BlockSpec((BM, D_A), lambda i, j: (i, 0), pipeline_mode=once)
    w_out_spec = pl.BlockSpec((None, D_A, BN), lambda i, j: (l, 0, j))
    return pl.pallas_call(
        _merge_kernel,
        grid=(N_ROW_TILES, D_MODEL // BN),
        in_specs=[pl.BlockSpec((BM, D_MODEL), lambda i, j: (i, 0), pipeline_mode=once),
                  z_spec, z_spec, z_spec,
                  gate_spec(0), gate_spec(1), gate_spec(2),
                  w_out_spec, w_out_spec, w_out_spec],
        out_specs=_tile_spec(),
        out_shape=jax.ShapeDtypeStruct((M_ROWS, D_MODEL), BF16),
        compiler_params=_params(("arbitrary", "arbitrary"), 56),
        name="merge",
    )(xn, za, zb, zc, w_gates, w_gates, w_gates, w_a, w_b, w_c)


OUT_BN = 512


def _outproj_kernel(m_ref, w_ref, y_ref):
    y_ref[...] = _wdot(m_ref[...], w_ref)


def _outproj(merged, w_o, l):
    return pl.pallas_call(
        _outproj_kernel,
        grid=(N_ROW_TILES, D_MODEL // OUT_BN),
        in_specs=[pl.BlockSpec((BM, D_MODEL), lambda i, j: (i, 0)),
                  pl.BlockSpec((None, D_MODEL, OUT_BN), lambda i, j: (l, 0, j))],
        out_specs=pl.BlockSpec((BM, OUT_BN), lambda i, j: (i, j)),
        out_shape=jax.ShapeDtypeStruct((M_ROWS, D_MODEL), F32),
        compiler_params=_params(("arbitrary", "arbitrary"), 52),
        name="outproj",
    )(merged, w_o)


def _postnorm_kernel(xp_ref, xs_ref, y_ref, gpost_ref, gpre_ref, xpo_ref, xso_ref, xn_ref):
    i = pl.program_id(0)

    def step(x_ref, xo_ref):
        x = x_ref[...] + _rms_scale(y_ref[...], gpost_ref[...])
        xo_ref[...] = x
        xn_ref[...] = _rms_scale(x, gpre_ref[...]).astype(BF16)

    @pl.when(i < N_PROMPT_TILES)
    def _():
        step(xp_ref, xpo_ref)

    @pl.when(i == N_PROMPT_TILES)
    def _():
        step(xs_ref, xso_ref)


def _postnorm(xp, xs, y, g_post, g_pre_next, in_place):
    row = lambda i: (i, 0)
    const = lambda i: (0, 0)
    return pl.pallas_call(
        _postnorm_kernel,
        grid=(N_PROMPT_TILES + 1,),
        in_specs=[pl.BlockSpec((ROW_T, D_MODEL), _prompt_tile),
                  pl.BlockSpec((ROW_T, D_MODEL), const),
                  pl.BlockSpec((ROW_T, D_MODEL), row),
                  pl.BlockSpec((1, D_MODEL), const),
                  pl.BlockSpec((1, D_MODEL), const)],
        out_specs=[pl.BlockSpec((ROW_T, D_MODEL), _prompt_tile),
                   pl.BlockSpec((ROW_T, D_MODEL), const),
                   pl.BlockSpec((ROW_T, D_MODEL), row)],
        out_shape=[jax.ShapeDtypeStruct((M_PROMPT, D_MODEL), F32),
                   jax.ShapeDtypeStruct((M_SAMPLE_PAD, D_MODEL), F32),
                   jax.ShapeDtypeStruct((M_ROWS, D_MODEL), BF16)],
        input_output_aliases={0: 0, 1: 1} if in_place else {},
        compiler_params=_params(("arbitrary",), 32),
        name="postnorm",
    )(xp, xs, y, g_post.reshape(1, D_MODEL), g_pre_next.reshape(1, D_MODEL))


def _heads_major(rows):
    return rows.reshape(DEC_BATCH, N_KV, GQ, HEAD_DIM).transpose(1, 0, 2, 3)


def kernel(x_prompt, x_sample, state_conv, cache_win_k, cache_win_v, w_in, conv_w, conv_b,
           conv_ln_g, conv_ln_b, attn_sink, sgu_ln_g, sgu_ln_b, sgu_w, sgu_b,
           w_a_out, w_b_out, w_c_out, w_o, g_pre, g_post):
    gate_cols = 3 * D_MODEL
    w_gates = _to_bf16(w_in, 512, gate_cols // 8, col_start=D_IN - gate_cols)
    w_a_b = _to_bf16(w_a_out, 512, D_MODEL)
    w_b_b = _to_bf16(w_b_out, 512, D_MODEL)
    w_c_b = _to_bf16(w_c_out, 512, D_MODEL)

    xp = x_prompt.reshape(M_PROMPT, D_MODEL)
    xs = jnp.pad(x_sample.reshape(DEC_BATCH, D_MODEL),
                 ((0, M_SAMPLE_PAD - DEC_BATCH), (0, 0)))
    xn = _prenorm(xp, xs, g_pre[0])

    s_rows = slice(M_PROMPT, M_PROMPT + DEC_BATCH)
    seq_end = [(b + 1) * SEQ for b in range(BATCH)]
    glu_p, kv_p, glu_s, kv_s, cv_s = [], [], [], [], []
    for l in range(DEPTH):
        glu, sa = _inproj_a(xn, w_in, l)
        q, kv, sb = _inproj_b(xn, w_in, l)
        ug, cv = _inproj_c(xn, w_in, l)

        za = _conv_prompt(glu, sa, conv_w[l], conv_b[l], conv_ln_g[l], conv_ln_b[l])
        za = _conv_sample(za, state_conv[l], glu, sa, conv_w[l], conv_b[l],
                          conv_ln_g[l], conv_ln_b[l])

        kv_new = kv[s_rows].reshape(DEC_BATCH, 2, N_KV, HEAD_DIM)
        zb = _attn_prompt(attn_sink[l], q, kv, sb)
        zb_s = _attn_sample(
            attn_sink[l], _heads_major(q[s_rows]), cache_win_k[l], cache_win_v[l],
            kv_new[:, 0].transpose(1, 0, 2), kv_new[:, 1].transpose(1, 0, 2),
            _heads_major(sb[s_rows]))
        zb_s = zb_s.transpose(1, 0, 2, 3).reshape(DEC_BATCH, D_B)
        zb_tail = jnp.pad(zb_s, ((0, M_SAMPLE_PAD - DEC_BATCH), (0, 0)))
        zb = lax.dynamic_update_slice(zb, zb_tail, (M_PROMPT, 0))

        zc, vn_s = _sgu(cv, ug, sgu_w[l], sgu_b[l].T, sgu_ln_g[l], sgu_ln_b[l])

        merged = _merge(xn, za, zb, zc, w_gates, w_a_b, w_b_b, w_c_b, l)
        y = _outproj(merged, w_o, l)
        xp, xs, xn = _postnorm(xp, xs, y, g_post[l], g_pre[(l + 1) % DEPTH], in_place=l > 0)

        glu_p.append(jnp.stack([glu[e - (CONV_W - 1):e] for e in seq_end]))
        kv_p.append(jnp.stack([kv[e - WINDOW:e] for e in seq_end]))
        glu_s.append(glu[s_rows])
        kv_s.append(kv_new)
        cv_s.append(vn_s[:DEC_BATCH])

    kv_p = jnp.stack(kv_p).reshape(DEPTH, BATCH, WINDOW, 2, N_KV, HEAD_DIM)
    kv_s = jnp.stack(kv_s)[:, :, None]
    conv_s = jnp.concatenate([state_conv[:, :, 1:], jnp.stack(glu_s)[:, :, None]], axis=2)
    k_s = jnp.concatenate([cache_win_k[:, :, 1:], kv_s[:, :, :, 0]], axis=2)
    v_s = jnp.concatenate([cache_win_v[:, :, 1:], kv_s[:, :, :, 1]], axis=2)
    y_prompt = xp.reshape(BATCH, SEQ, D_MODEL)
    y_sample = xs[:DEC_BATCH].reshape(DEC_BATCH, 1, D_MODEL)
    return (y_prompt, y_sample, jnp.stack(glu_p), kv_p[:, :, :, 0], kv_p[:, :, :, 1],
            conv_s, k_s, v_s, jnp.stack(cv_s)[:, :, None])
```

```python
import jax
import jax.numpy as jnp
from jax import lax
from jax.experimental import pallas as pl
from jax.experimental.pallas import tpu as pltpu

F32 = jnp.float32
BF16 = jnp.bfloat16

D_MODEL = 4096
BATCH = 4
SEQ = 2048
DEPTH = 4
DEC_BATCH = 32
EPS = 1e-6
D_A = D_MODEL // 2
CONV_W = 31
N_HEADS = 32
N_KV = 4
HEAD_DIM = 64
GQ = N_HEADS // N_KV
D_B = N_HEADS * HEAD_DIM
KV_W = N_KV * HEAD_DIM
WINDOW = 128
D_C = D_MODEL // 2
CHUNK = 128
SGU_GROUPS = 16
D_IN = 3 * D_A + 2 * D_B + 2 * KV_W + 3 * D_C + 3 * D_MODEL

LANES = 128
MIB = 1024 * 1024

M_PROMPT = BATCH * SEQ
M_SAMPLE_PAD = 128
M_ROWS = M_PROMPT + M_SAMPLE_PAD
BM = 1040
BN = 256
N_ROW_TILES = M_ROWS // BM

OFF_A_LIN = 0
OFF_A_GLU = D_A // BN
OFF_A_GATE = 2 * D_A // BN
OFF_B = 3 * D_A // BN
OFF_C_U = (3 * D_A + 2 * D_B + 2 * KV_W) // BN
OFF_C_V = OFF_C_U + D_C // BN
OFF_C_GATE = OFF_C_V + D_C // BN

CONV_T = 128
CONV_HALO = 32
CONV_LC = 128
SUBLANES = 8
CONV_SH_ROWS = CONV_HALO + CONV_T - SUBLANES


def _params(semantics, vmem_mib):
    return pltpu.CompilerParams(dimension_semantics=semantics,
                                vmem_limit_bytes=vmem_mib * MIB)


def _dot(a, b):
    return jnp.dot(a, b, preferred_element_type=F32)


def _wdot(a, w_ref):
    return _dot(a, w_ref[...].astype(BF16))


def _sigmoid(x):
    return 1.0 / (1.0 + jnp.exp(-x))


def _silu(x):
    return x * _sigmoid(x)


def _cast_kernel(w_ref, o_ref):
    o_ref[...] = w_ref[...].astype(BF16)


def _to_bf16(w, layer, rows, cols, col_start=0):
    _, k, n = w.shape
    first = col_start // cols
    return pl.pallas_call(
        _cast_kernel,
        grid=(k // rows, (n - col_start) // cols),
        in_specs=[pl.BlockSpec((None, rows, cols), lambda i, j: (layer, i, first + j))],
        out_specs=pl.BlockSpec((rows, cols), lambda i, j: (i, j)),
        out_shape=jax.ShapeDtypeStruct((k, n - col_start), BF16),
        compiler_params=_params(("arbitrary",) * 2, 48),
        name="cast_weights",
    )(w)


ROW_T = 128
N_PROMPT_TILES = M_PROMPT // ROW_T


def _rms_scale(x, g):
    ms = jnp.mean(x * x, axis=-1, keepdims=True)
    return x * lax.rsqrt(ms + EPS) * g


def _prompt_tile(i):
    return (jnp.minimum(i, N_PROMPT_TILES - 1), 0)


def _prenorm_kernel(xp_ref, xs_ref, g_ref, o_ref):
    i = pl.program_id(0)

    @pl.when(i < N_PROMPT_TILES)
    def _():
        o_ref[...] = _rms_scale(xp_ref[...], g_ref[...]).astype(BF16)

    @pl.when(i == N_PROMPT_TILES)
    def _():
        o_ref[...] = _rms_scale(xs_ref[...], g_ref[...]).astype(BF16)


def _prenorm(xp, xs, g):
    const = lambda i: (0, 0)
    return pl.pallas_call(
        _prenorm_kernel,
        grid=(N_PROMPT_TILES + 1,),
        in_specs=[pl.BlockSpec((ROW_T, D_MODEL), _prompt_tile),
                  pl.BlockSpec((ROW_T, D_MODEL), const),
                  pl.BlockSpec((1, D_MODEL), const)],
        out_specs=pl.BlockSpec((ROW_T, D_MODEL), lambda i: (i, 0)),
        out_shape=jax.ShapeDtypeStruct((M_ROWS, D_MODEL), BF16),
        compiler_params=_params(("arbitrary",), 32),
        name="prenorm",
    )(xp, xs, g.reshape(1, D_MODEL))


def _w_in_spec(l, off):
    return pl.BlockSpec((None, D_MODEL, BN), lambda i, j: (l, 0, off + j))


def _xn_spec():
    return pl.BlockSpec((BM, D_MODEL), lambda i, j: (i, 0))


def _tile_spec():
    return pl.BlockSpec((BM, BN), lambda i, j: (i, j))


def _inproj_a_kernel(xn_ref, wl_ref, wg_ref, wt_ref, glu_ref, sa_ref):
    xn = xn_ref[...]
    lin = _wdot(xn, wl_ref)
    gl = _wdot(xn, wg_ref)
    glu_ref[...] = lin * _sigmoid(gl)
    gate = _wdot(xn, wt_ref)
    sa_ref[...] = _silu(gate).astype(BF16)


def _inproj_a(xn, w_in, l):
    return pl.pallas_call(
        _inproj_a_kernel,
        grid=(N_ROW_TILES, D_A // BN),
        in_specs=[_xn_spec(), _w_in_spec(l, OFF_A_LIN), _w_in_spec(l, OFF_A_GLU),
                  _w_in_spec(l, OFF_A_GATE)],
        out_specs=[_tile_spec(), _tile_spec()],
        out_shape=[jax.ShapeDtypeStruct((M_ROWS, D_A), F32),
                   jax.ShapeDtypeStruct((M_ROWS, D_A), BF16)],
        compiler_params=_params(("arbitrary", "arbitrary"), 56),
        name="inproj_a",
    )(xn, w_in, w_in, w_in)


BN_B = 2 * KV_W
NB_Q = D_B // BN_B
NB_KV = 2 * KV_W // BN_B
NB_B = 2 * NB_Q + NB_KV


def _inproj_b_kernel(xn_ref, w_ref, q_ref, kv_ref, sb_ref):
    j = pl.program_id(1)
    acc = _wdot(xn_ref[...], w_ref)

    @pl.when(j < NB_Q)
    def _():
        q_ref[...] = (acc * (HEAD_DIM ** -0.5)).astype(BF16)

    @pl.when(jnp.logical_and(j >= NB_Q, j < NB_Q + NB_KV))
    def _():
        kv_ref[...] = acc

    @pl.when(j >= NB_Q + NB_KV)
    def _():
        sb_ref[...] = _silu(acc).astype(BF16)


def _inproj_b(xn, w_in, l):
    kv_map = lambda i, j: (i, jnp.clip(j - NB_Q, 0, NB_KV - 1))
    return pl.pallas_call(
        _inproj_b_kernel,
        grid=(N_ROW_TILES, NB_B),
        in_specs=[_xn_spec(),
                  pl.BlockSpec((None, D_MODEL, BN_B),
                               lambda i, j: (l, 0, OFF_B * BN // BN_B + j))],
        out_specs=[pl.BlockSpec((BM, BN_B), lambda i, j: (i, jnp.minimum(j, NB_Q - 1))),
                   pl.BlockSpec((BM, BN_B), kv_map),
                   pl.BlockSpec((BM, BN_B),
                                lambda i, j: (i, jnp.clip(j - NB_Q - NB_KV, 0, NB_Q - 1)))],
        out_shape=[jax.ShapeDtypeStruct((M_ROWS, D_B), BF16),
                   jax.ShapeDtypeStruct((M_ROWS, 2 * KV_W), F32),
                   jax.ShapeDtypeStruct((M_ROWS, D_B), BF16)],
        compiler_params=_params(("arbitrary", "arbitrary"), 56),
        name="inproj_b",
    )(xn, w_in)


def _inproj_c_kernel(xn_ref, wu_ref, wv_ref, wt_ref, ug_ref, cv_ref):
    xn = xn_ref[...]
    u = _wdot(xn, wu_ref)
    gate = _wdot(xn, wt_ref)
    ug_ref[...] = (u * _silu(gate)).astype(BF16)
    cv_ref[...] = _wdot(xn, wv_ref)


def _inproj_c(xn, w_in, l):
    return pl.pallas_call(
        _inproj_c_kernel,
        grid=(N_ROW_TILES, D_C // BN),
        in_specs=[_xn_spec(), _w_in_spec(l, OFF_C_U), _w_in_spec(l, OFF_C_V),
                  _w_in_spec(l, OFF_C_GATE)],
        out_specs=[_tile_spec(), _tile_spec()],
        out_shape=[jax.ShapeDtypeStruct((M_ROWS, D_C), BF16),
                   jax.ShapeDtypeStruct((M_ROWS, D_C), F32)],
        compiler_params=_params(("arbitrary", "arbitrary"), 56),
        name="inproj_c",
    )(xn, w_in, w_in, w_in)


def _ln_silu_gate(y, g, b, gate):
    mu = jnp.mean(y, axis=-1, keepdims=True)
    yc = y - mu
    var = jnp.mean(yc * yc, axis=-1, keepdims=True)
    yn = yc * lax.rsqrt(var + EPS) * g + b
    return (_silu(yn) * gate.astype(F32)).astype(BF16)


def _conv_prompt_kernel(cur_ref, prev_ref, sa_ref, w_ref, cb_ref, g_ref, b_ref, out_ref,
                        ext_ref, sh_ref, y_ref):
    t = pl.program_id(1)
    ext_ref[0:CONV_HALO, :] = jnp.where(t == 0, 0.0, prev_ref[...])
    ext_ref[CONV_HALO:CONV_HALO + CONV_T, :] = cur_ref[...]
    for s in range(1, SUBLANES):
        sh_ref[s - 1] = ext_ref[s:s + CONV_SH_ROWS, :]
    first = CONV_HALO - (CONV_W - 1)
    groups = range(0, CONV_T, SUBLANES)
    for c0 in range(0, D_A, CONV_LC):
        cs = slice(c0, c0 + CONV_LC)
        bias = jnp.broadcast_to(cb_ref[:, cs], (SUBLANES, CONV_LC))
        acc = [bias for _ in groups]
        for j in range(CONV_W):
            phase = (first + j) % SUBLANES
            base = first + j - phase
            wj = jnp.broadcast_to(w_ref[j:j + 1, cs], (SUBLANES, CONV_LC))
            for gi, r0 in enumerate(groups):
                rows = slice(base + r0, base + r0 + SUBLANES)
                window = ext_ref[rows, cs] if phase == 0 else sh_ref[phase - 1, rows, cs]
                acc[gi] = acc[gi] + wj * window
        for gi, r0 in enumerate(groups):
            y_ref[r0:r0 + SUBLANES, cs] = acc[gi]
    out_ref[...] = _ln_silu_gate(y_ref[...], g_ref[...], b_ref[...], sa_ref[...])


def _conv_prompt(glu, sa, conv_w, conv_b, ln_g, ln_b):
    nt = SEQ // CONV_T
    halo_per_tile = CONV_T // CONV_HALO

    def prev_map(b, t):
        return (jnp.maximum((b * nt + t) * halo_per_tile - 1, 0), 0)

    row = lambda b, t: (b * nt + t, 0)
    const = lambda b, t: (0, 0)
    return pl.pallas_call(
        _conv_prompt_kernel,
        grid=(BATCH, nt),
        in_specs=[pl.BlockSpec((CONV_T, D_A), row),
                  pl.BlockSpec((CONV_HALO, D_A), prev_map),
                  pl.BlockSpec((CONV_T, D_A), row),
                  pl.BlockSpec((CONV_W, D_A), const),
                  pl.BlockSpec((1, D_A), const),
                  pl.BlockSpec((1, D_A), const),
                  pl.BlockSpec((1, D_A), const)],
        out_specs=pl.BlockSpec((CONV_T, D_A), row),
        out_shape=jax.ShapeDtypeStruct((M_ROWS, D_A), BF16),
        scratch_shapes=[pltpu.VMEM((CONV_HALO + CONV_T, D_A), F32),
                        pltpu.VMEM((SUBLANES - 1, CONV_SH_ROWS, D_A), F32),
                        pltpu.VMEM((CONV_T, D_A), F32)],
        compiler_params=_params(("arbitrary", "arbitrary"), 40),
        name="conv_prompt",
    )(glu, glu, sa, conv_w, conv_b.reshape(1, D_A), ln_g.reshape(1, D_A),
      ln_b.reshape(1, D_A))


def _conv_sample_kernel(za_in_ref, st_ref, glu_ref, sa_ref, w_ref, cb_ref, g_ref, b_ref,
                        out_ref):
    del za_in_ref
    acc = cb_ref[...] + w_ref[CONV_W - 1:CONV_W, :] * glu_ref[0:DEC_BATCH, :]
    for j in range(CONV_W - 1):
        acc = acc + w_ref[j:j + 1, :] * st_ref[:, j, :]
    out_ref[0:DEC_BATCH, :] = _ln_silu_gate(acc, g_ref[...], b_ref[...],
                                            sa_ref[0:DEC_BATCH, :])
    out_ref[DEC_BATCH:, :] = jnp.zeros((M_SAMPLE_PAD - DEC_BATCH, D_A), BF16)


def _conv_sample(za, state, glu, sa, conv_w, conv_b, ln_g, ln_b):
    sample_tile = lambda i: (M_PROMPT // M_SAMPLE_PAD, 0)
    const = lambda i: (0, 0)
    return pl.pallas_call(
        _conv_sample_kernel,
        grid=(1,),
        in_specs=[pl.BlockSpec(memory_space=pl.ANY),
                  pl.BlockSpec((DEC_BATCH, CONV_W - 1, D_A), lambda i: (0, 0, 0)),
                  pl.BlockSpec((M_SAMPLE_PAD, D_A), sample_tile),
                  pl.BlockSpec((M_SAMPLE_PAD, D_A), sample_tile),
                  pl.BlockSpec((CONV_W, D_A), const),
                  pl.BlockSpec((1, D_A), const),
                  pl.BlockSpec((1, D_A), const),
                  pl.BlockSpec((1, D_A), const)],
        out_specs=pl.BlockSpec((M_SAMPLE_PAD, D_A), sample_tile),
        out_shape=jax.ShapeDtypeStruct((M_ROWS, D_A), BF16),
        input_output_aliases={0: 0},
        compiler_params=_params(("arbitrary",), 40),
        name="conv_sample",
    )(za, state, glu, sa, conv_w, conv_b.reshape(1, D_A), ln_g.reshape(1, D_A),
      ln_b.reshape(1, D_A))


def _softmax_sink_keys_major(s, sink):
    m = jnp.maximum(jnp.max(s, axis=0, keepdims=True), sink)
    p = jnp.exp(s - m)
    denom = jnp.sum(p, axis=0, keepdims=True) + jnp.exp(sink - m)
    return p / denom


def _attn_prompt_kernel(sink_ref, q_ref, kp_ref, kc_ref, vp_ref, vc_ref, sb_ref, out_ref,
                        s_ref, p_ref, vbd_ref):
    n = pl.program_id(1)
    nk = 2 * CHUNK
    kc_i = lax.broadcasted_iota(jnp.int32, (nk, CHUNK), 0)
    qi = lax.broadcasted_iota(jnp.int32, (nk, CHUNK), 1)
    mask = jnp.logical_and(kc_i >= qi, kc_i <= qi + WINDOW)
    mask = jnp.logical_and(mask, jnp.logical_or(n > 0, kc_i >= CHUNK))
    lane = lax.broadcasted_iota(jnp.int32, (nk, LANES), 1)
    lo = lane < HEAD_DIM
    zeros_t = jnp.zeros((HEAD_DIM, nk), F32)
    for slab in range(KV_W // LANES):
        ls = slice(slab * LANES, (slab + 1) * LANES)
        k_two = jnp.concatenate([kp_ref[:, ls], kc_ref[:, ls]], axis=0)
        v_two_t = jnp.concatenate([vp_ref[:, ls], vc_ref[:, ls]], axis=0).T
        k_sw = pltpu.roll(k_two, HEAD_DIM, axis=1)
        for half in range(LANES // HEAD_DIM):
            kv = slab * (LANES // HEAD_DIM) + half
            k_lo, k_hi = (k_two, k_sw) if half == 0 else (k_sw, k_two)
            kbd = jnp.concatenate([jnp.where(lo, k_lo, 0.0), jnp.where(lo, 0.0, k_hi)],
                                  axis=0).astype(BF16)
            vt = v_two_t[half * HEAD_DIM:(half + 1) * HEAD_DIM]
            vbd_ref[kv] = jnp.concatenate(
                [jnp.concatenate([vt, zeros_t], axis=1),
                 jnp.concatenate([zeros_t, vt], axis=1)], axis=0).astype(BF16)
            for pair in range(GQ // 2):
                h0 = kv * GQ + 2 * pair
                cs = slice(h0 * HEAD_DIM, h0 * HEAD_DIM + LANES)
                s_ref[h0 // 2] = lax.dot_general(kbd, q_ref[:, cs], (((1,), (1,)), ((), ())),
                                                 preferred_element_type=F32)
    for h in range(N_HEADS):
        rows = slice((h % 2) * nk, (h % 2 + 1) * nk)
        s = jnp.where(mask, s_ref[h // 2, rows, :], -jnp.inf)
        p_ref[h // 2, rows, :] = _softmax_sink_keys_major(s, sink_ref[h]).astype(BF16)
    for kv in range(N_KV):
        for pair in range(GQ // 2):
            h0 = kv * GQ + 2 * pair
            cs = slice(h0 * HEAD_DIM, h0 * HEAD_DIM + LANES)
            o = _dot(vbd_ref[kv], p_ref[h0 // 2]).T
            out_ref[:, cs] = (o * sb_ref[:, cs].astype(F32)).astype(BF16)


def _attn_prompt(sink, q, kvb, sb):
    nb = SEQ // CHUNK
    row = lambda b, n: (b * nb + n, 0)
    prev_k = lambda b, n: (b * nb + jnp.maximum(n - 1, 0), 0)
    prev_v = lambda b, n: (b * nb + jnp.maximum(n - 1, 0), 1)
    cur_v = lambda b, n: (b * nb + n, 1)
    return pl.pallas_call(
        _attn_prompt_kernel,
        grid=(BATCH, nb),
        in_specs=[pl.BlockSpec(memory_space=pltpu.SMEM),
                  pl.BlockSpec((CHUNK, D_B), row),
                  pl.BlockSpec((CHUNK, KV_W), prev_k),
                  pl.BlockSpec((CHUNK, KV_W), row),
                  pl.BlockSpec((CHUNK, KV_W), prev_v),
                  pl.BlockSpec((CHUNK, KV_W), cur_v),
                  pl.BlockSpec((CHUNK, D_B), row)],
        out_specs=pl.BlockSpec((CHUNK, D_B), row),
        out_shape=jax.ShapeDtypeStruct((M_ROWS, D_B), BF16),
        scratch_shapes=[pltpu.VMEM((N_HEADS // 2, 4 * CHUNK, CHUNK), F32),
                        pltpu.VMEM((N_HEADS // 2, 4 * CHUNK, CHUNK), BF16),
                        pltpu.VMEM((N_KV, LANES, 4 * CHUNK), BF16)],
        compiler_params=_params(("arbitrary", "arbitrary"), 32),
        name="attn_prompt",
    )(sink, q, kvb, kvb, kvb, kvb, sb)


def _attn_sample_kernel(sink_ref, q_ref, kc_ref, vc_ref, kn_ref, vn_ref, sb_ref, out_ref):
    gi = lax.broadcasted_iota(jnp.int32, (1, GQ, 1), 1)
    for kv in range(N_KV):
        q = q_ref[kv]
        qf = q.astype(F32)
        kc = kc_ref[:, :, kv, :].astype(BF16)
        vc = vc_ref[:, :, kv, :].astype(BF16)
        kn = kn_ref[kv].astype(BF16).astype(F32)
        vn = vn_ref[kv].astype(BF16).astype(F32)
        sink = jnp.zeros((1, GQ, 1), F32)
        for g in range(GQ):
            sink = jnp.where(gi == g, sink_ref[kv * GQ + g], sink)
        s_c = lax.dot_general(q, kc, (((2,), (2,)), ((0,), (0,))),
                              preferred_element_type=F32)
        s_n = jnp.sum(qf * kn[:, None, :], axis=-1, keepdims=True)
        m = jnp.maximum(jnp.maximum(jnp.max(s_c, axis=-1, keepdims=True), s_n), sink)
        p_c = jnp.exp(s_c - m)
        p_n = jnp.exp(s_n - m)
        denom = jnp.sum(p_c, axis=-1, keepdims=True) + p_n + jnp.exp(sink - m)
        p_c = (p_c / denom).astype(BF16)
        p_n = (p_n / denom).astype(BF16).astype(F32)
        o = lax.dot_general(p_c, vc, (((2,), (1,)), ((0,), (0,))),
                            preferred_element_type=F32)
        o = o + p_n * vn[:, None, :]
        out_ref[kv] = (o * sb_ref[kv].astype(F32)).astype(BF16)


def _attn_sample(sink, q_s, k_cache, v_cache, k_new, v_new, sb_s):
    return pl.pallas_call(
        _attn_sample_kernel,
        in_specs=[pl.BlockSpec(memory_space=pltpu.SMEM)]
        + [pl.BlockSpec(memory_space=pltpu.VMEM)] * 6,
        out_specs=pl.BlockSpec(memory_space=pltpu.VMEM),
        out_shape=jax.ShapeDtypeStruct((N_KV, DEC_BATCH, GQ, HEAD_DIM), BF16),
        compiler_params=pltpu.CompilerParams(vmem_limit_bytes=48 * MIB),
        name="attn_sample",
    )(sink, q_s, k_cache, v_cache, k_new, v_new, sb_s)


N_CHUNKS = M_PROMPT // CHUNK


def _sgu_kernel(cv_ref, ug_ref, w_ref, bt_ref, g_ref, b_ref, out_ref, vn_ref):
    c = pl.program_id(0)
    is_sample = c == N_CHUNKS
    x = cv_ref[...]
    mu = jnp.mean(x, axis=-1, keepdims=True)
    xc = x - mu
    var = jnp.mean(xc * xc, axis=-1, keepdims=True)
    vn = xc * lax.rsqrt(var + EPS) * g_ref[...] + b_ref[...]

    @pl.when(is_sample)
    def _():
        vn_ref[...] = vn

    vnb = vn.astype(BF16)
    ri = lax.broadcasted_iota(jnp.int32, (CHUNK, CHUNK), 0)
    ci = lax.broadcasted_iota(jnp.int32, (CHUNK, CHUNK), 1)
    tril = ri >= ci
    diag = ri == ci
    for g in range(SGU_GROUPS):
        gs = slice(g * CHUNK, (g + 1) * CHUNK)
        w = w_ref[g]
        w_first = jnp.broadcast_to(w[0:1, 0:1], (CHUNK, CHUNK))
        wm = jnp.where(is_sample, jnp.where(diag, w_first, 0.0), jnp.where(tril, w, 0.0))
        bias = bt_ref[:, g:g + 1]
        bias = jnp.where(is_sample, jnp.broadcast_to(bias[0:1, :], (CHUNK, 1)), bias)
        z = _dot(wm.astype(BF16), vnb[:, gs]) + bias
        out_ref[:, gs] = (ug_ref[:, gs].astype(F32) * z).astype(BF16)


def _sgu(cv, ug, sgu_w, sgu_bt, ln_g, ln_b):
    row = lambda c: (c, 0)
    const = lambda c: (0, 0)
    return pl.pallas_call(
        _sgu_kernel,
        grid=(N_CHUNKS + 1,),
        in_specs=[pl.BlockSpec((CHUNK, D_C), row),
                  pl.BlockSpec((CHUNK, D_C), row),
                  pl.BlockSpec((SGU_GROUPS, CHUNK, CHUNK), lambda c: (0, 0, 0)),
                  pl.BlockSpec((CHUNK, SGU_GROUPS), const),
                  pl.BlockSpec((1, D_C), const),
                  pl.BlockSpec((1, D_C), const)],
        out_specs=[pl.BlockSpec((CHUNK, D_C), row),
                   pl.BlockSpec((CHUNK, D_C), const)],
        out_shape=[jax.ShapeDtypeStruct((M_ROWS, D_C), BF16),
                   jax.ShapeDtypeStruct((CHUNK, D_C), F32)],
        compiler_params=_params(("arbitrary",), 32),
        name="sgu",
    )(cv, ug, sgu_w, sgu_bt, ln_g.reshape(1, D_C), ln_b.reshape(1, D_C))


def _merge_kernel(xn_ref, za_ref, zb_ref, zc_ref, wma_ref, wmb_ref, wmc_ref,
                  wa_ref, wb_ref, wc_ref, out_ref):
    xn = xn_ref[...]
    acc = _sigmoid(_dot(xn, wma_ref[...])) * _dot(za_ref[...], wa_ref[...])
    acc = acc + _sigmoid(_dot(xn, wmb_ref[...])) * _dot(zb_ref[...], wb_ref[...])
    acc = acc + _sigmoid(_dot(xn, wmc_ref[...])) * _dot(zc_ref[...], wc_ref[...])
    out_ref[...] = acc.astype(BF16)


def _merge(xn, za, zb, zc, w_gates, w_a, w_b, w_c):
    gate_spec = lambda g: pl.BlockSpec((D_MODEL, BN),
                                       lambda i, j: (0, g * (D_MODEL // BN) + j))
    once = pl.Buffered(1)
    z_spec = pl.BlockSpec((BM, D_A), lambda i, j: (i, 0), pipeline_mode=once)
    w_out_spec = pl.BlockSpec((D_A, BN), lambda i, j: (0, j))
    return pl.pallas_call(
        _merge_kernel,
        grid=(N_ROW_TILES, D_MODEL // BN),
        in_specs=[pl.BlockSpec((BM, D_MODEL), lambda i, j: (i, 0), pipeline_mode=once),
                  z_spec, z_spec, z_spec,
                  gate_spec(0), gate_spec(1), gate_spec(2),
                  w_out_spec, w_out_spec, w_out_spec],
        out_specs=_tile_spec(),
        out_shape=jax.ShapeDtypeStruct((M_ROWS, D_MODEL), BF16),
        compiler_params=_params(("arbitrary", "arbitrary"), 56),
        name="merge",
    )(xn, za, zb, zc, w_gates, w_gates, w_gates, w_a, w_b, w_c)


OUT_BN = 512


def _outproj_kernel(m_ref, w_ref, *cast_refs):
    n_cast = len(cast_refs) // 2
    y_ref = cast_refs[n_cast]
    y_ref[...] = _wdot(m_ref[...], w_ref)
    for src_ref, dst_ref in zip(cast_refs[:n_cast], cast_refs[n_cast + 1:]):
        dst_ref[...] = src_ref[...].astype(BF16)


GATE_COLS = 3 * D_MODEL
OUT_COL_TILES = D_MODEL // OUT_BN
GATE_CAST_TILE = (D_MODEL // N_ROW_TILES, GATE_COLS // OUT_COL_TILES)
BRANCH_CAST_TILE = (D_A // N_ROW_TILES, D_MODEL // OUT_COL_TILES)


def _outproj(merged, w_o, l, next_merge_weights=None):
    in_specs = [pl.BlockSpec((BM, D_MODEL), lambda i, j: (i, 0)),
                pl.BlockSpec((None, D_MODEL, OUT_BN), lambda i, j: (l, 0, j))]
    out_specs = [pl.BlockSpec((BM, OUT_BN), lambda i, j: (i, j))]
    out_shape = [jax.ShapeDtypeStruct((M_ROWS, D_MODEL), F32)]
    operands = [merged, w_o]
    if next_merge_weights is not None:
        w_in, w_a, w_b, w_c = next_merge_weights
        first_gate_tile = (D_IN - GATE_COLS) // GATE_CAST_TILE[1]
        in_specs.append(pl.BlockSpec((None,) + GATE_CAST_TILE,
                                     lambda i, j: (l + 1, i, first_gate_tile + j)))
        out_specs.append(pl.BlockSpec(GATE_CAST_TILE, lambda i, j: (i, j)))
        out_shape.append(jax.ShapeDtypeStruct((D_MODEL, GATE_COLS), BF16))
        for _ in range(3):
            in_specs.append(pl.BlockSpec((None,) + BRANCH_CAST_TILE,
                                         lambda i, j: (l + 1, i, j)))
            out_specs.append(pl.BlockSpec(BRANCH_CAST_TILE, lambda i, j: (i, j)))
            out_shape.append(jax.ShapeDtypeStruct((D_A, D_MODEL), BF16))
        operands += [w_in, w_a, w_b, w_c]
    return pl.pallas_call(
        _outproj_kernel,
        grid=(N_ROW_TILES, OUT_COL_TILES),
        in_specs=in_specs,
        out_specs=out_specs,
        out_shape=out_shape,
        compiler_params=_params(("arbitrary", "arbitrary"), 58),
        name="outproj",
    )(*operands)


def _postnorm_kernel(xp_ref, xs_ref, y_ref, gpost_ref, gpre_ref, xpo_ref, xso_ref, xn_ref):
    i = pl.program_id(0)

    def step(x_ref, xo_ref):
        x = x_ref[...] + _rms_scale(y_ref[...], gpost_ref[...])
        xo_ref[...] = x
        xn_ref[...] = _rms_scale(x, gpre_ref[...]).astype(BF16)

    @pl.when(i < N_PROMPT_TILES)
    def _():
        step(xp_ref, xpo_ref)

    @pl.when(i == N_PROMPT_TILES)
    def _():
        step(xs_ref, xso_ref)


def _postnorm(xp, xs, y, g_post, g_pre_next, in_place):
    row = lambda i: (i, 0)
    const = lambda i: (0, 0)
    return pl.pallas_call(
        _postnorm_kernel,
        grid=(N_PROMPT_TILES + 1,),
        in_specs=[pl.BlockSpec((ROW_T, D_MODEL), _prompt_tile),
                  pl.BlockSpec((ROW_T, D_MODEL), const),
                  pl.BlockSpec((ROW_T, D_MODEL), row),
                  pl.BlockSpec((1, D_MODEL), const),
                  pl.BlockSpec((1, D_MODEL), const)],
        out_specs=[pl.BlockSpec((ROW_T, D_MODEL), _prompt_tile),
                   pl.BlockSpec((ROW_T, D_MODEL), const),
                   pl.BlockSpec((ROW_T, D_MODEL), row)],
        out_shape=[jax.ShapeDtypeStruct((M_PROMPT, D_MODEL), F32),
                   jax.ShapeDtypeStruct((M_SAMPLE_PAD, D_MODEL), F32),
                   jax.ShapeDtypeStruct((M_ROWS, D_MODEL), BF16)],
        input_output_aliases={0: 0, 1: 1} if in_place else {},
        compiler_params=_params(("arbitrary",), 32),
        name="postnorm",
    )(xp, xs, y, g_post.reshape(1, D_MODEL), g_pre_next.reshape(1, D_MODEL))


def _heads_major(rows):
    return rows.reshape(DEC_BATCH, N_KV, GQ, HEAD_DIM).transpose(1, 0, 2, 3)


def kernel(x_prompt, x_sample, state_conv, cache_win_k, cache_win_v, w_in, conv_w, conv_b,
           conv_ln_g, conv_ln_b, attn_sink, sgu_ln_g, sgu_ln_b, sgu_w, sgu_b,
           w_a_out, w_b_out, w_c_out, w_o, g_pre, g_post):
    merge_w = [_to_bf16(w_in, 0, 512, GATE_COLS // 8, col_start=D_IN - GATE_COLS),
               _to_bf16(w_a_out, 0, 512, D_MODEL), _to_bf16(w_b_out, 0, 512, D_MODEL),
               _to_bf16(w_c_out, 0, 512, D_MODEL)]

    xp = x_prompt.reshape(M_PROMPT, D_MODEL)
    xs = jnp.pad(x_sample.reshape(DEC_BATCH, D_MODEL),
                 ((0, M_SAMPLE_PAD - DEC_BATCH), (0, 0)))
    xn = _prenorm(xp, xs, g_pre[0])

    s_rows = slice(M_PROMPT, M_PROMPT + DEC_BATCH)
    seq_end = [(b + 1) * SEQ for b in range(BATCH)]
    glu_p, kv_p, glu_s, kv_s, cv_s = [], [], [], [], []
    for l in range(DEPTH):
        glu, sa = _inproj_a(xn, w_in, l)
        q, kv, sb = _inproj_b(xn, w_in, l)
        ug, cv = _inproj_c(xn, w_in, l)

        za = _conv_prompt(glu, sa, conv_w[l], conv_b[l], conv_ln_g[l], conv_ln_b[l])
        za = _conv_sample(za, state_conv[l], glu, sa, conv_w[l], conv_b[l],
                          conv_ln_g[l], conv_ln_b[l])

        kv_new = kv[s_rows].reshape(DEC_BATCH, 2, N_KV, HEAD_DIM)
        zb = _attn_prompt(attn_sink[l], q, kv, sb)
        zb_s = _attn_sample(
            attn_sink[l], _heads_major(q[s_rows]), cache_win_k[l], cache_win_v[l],
            kv_new[:, 0].transpose(1, 0, 2), kv_new[:, 1].transpose(1, 0, 2),
            _heads_major(sb[s_rows]))
        zb_s = zb_s.transpose(1, 0, 2, 3).reshape(DEC_BATCH, D_B)
        zb_tail = jnp.pad(zb_s, ((0, M_SAMPLE_PAD - DEC_BATCH), (0, 0)))
        zb = lax.dynamic_update_slice(zb, zb_tail, (M_PROMPT, 0))

        zc, vn_s = _sgu(cv, ug, sgu_w[l], sgu_b[l].T, sgu_ln_g[l], sgu_ln_b[l])

        merged = _merge(xn, za, zb, zc, *merge_w)
        if l + 1 < DEPTH:
            y, *merge_w = _outproj(merged, w_o, l, (w_in, w_a_out, w_b_out, w_c_out))
        else:
            y, = _outproj(merged, w_o, l)
        xp, xs, xn = _postnorm(xp, xs, y, g_post[l], g_pre[(l + 1) % DEPTH], in_place=l > 0)

        glu_p.append(jnp.stack([glu[e - (CONV_W - 1):e] for e in seq_end]))
        kv_p.append(jnp.stack([kv[e - WINDOW:e] for e in seq_end]))
        glu_s.append(glu[s_rows])
        kv_s.append(kv_new)
        cv_s.append(vn_s[:DEC_BATCH])

    kv_p = jnp.stack(kv_p).reshape(DEPTH, BATCH, WINDOW, 2, N_KV, HEAD_DIM)
    kv_s = jnp.stack(kv_s)[:, :, None]
    conv_s = jnp.concatenate([state_conv[:, :, 1:], jnp.stack(glu_s)[:, :, None]], axis=2)
    k_s = jnp.concatenate([cache_win_k[:, :, 1:], kv_s[:, :, :, 0]], axis=2)
    v_s = jnp.concatenate([cache_win_v[:, :, 1:], kv_s[:, :, :, 1]], axis=2)
    y_prompt = xp.reshape(BATCH, SEQ, D_MODEL)
    y_sample = xs[:DEC_BATCH].reshape(DEC_BATCH, 1, D_MODEL)
    return (y_prompt, y_sample, jnp.stack(glu_p), kv_p[:, :, :, 0], kv_p[:, :, :, 1],
            conv_s, k_s, v_s, jnp.stack(cv_s)[:, :, None])
```

```python
import jax
import jax.numpy as jnp
from jax import lax
from jax.experimental import pallas as pl
from jax.experimental.pallas import tpu as pltpu

F32 = jnp.float32
BF16 = jnp.bfloat16

D_MODEL = 4096
BATCH = 4
SEQ = 2048
DEPTH = 4
DEC_BATCH = 32
EPS = 1e-6
D_A = D_MODEL // 2
CONV_W = 31
N_HEADS = 32
N_KV = 4
HEAD_DIM = 64
GQ = N_HEADS // N_KV
D_B = N_HEADS * HEAD_DIM
KV_W = N_KV * HEAD_DIM
WINDOW = 128
D_C = D_MODEL // 2
CHUNK = 128
SGU_GROUPS = 16
D_IN = 3 * D_A + 2 * D_B + 2 * KV_W + 3 * D_C + 3 * D_MODEL

LANES = 128
MIB = 1024 * 1024

M_PROMPT = BATCH * SEQ
M_SAMPLE_PAD = 128
M_ROWS = M_PROMPT + M_SAMPLE_PAD
BM = 1040
BN = 256
N_ROW_TILES = M_ROWS // BM

OFF_A_LIN = 0
OFF_A_GLU = D_A // BN
OFF_A_GATE = 2 * D_A // BN
OFF_B = 3 * D_A // BN
OFF_C_U = (3 * D_A + 2 * D_B + 2 * KV_W) // BN
OFF_C_V = OFF_C_U + D_C // BN
OFF_C_GATE = OFF_C_V + D_C // BN

CONV_T = 128
CONV_HALO = 32
CONV_LC = 128
SUBLANES = 8
CONV_SH_ROWS = CONV_HALO + CONV_T - SUBLANES


def _params(semantics, vmem_mib):
    return pltpu.CompilerParams(dimension_semantics=semantics,
                                vmem_limit_bytes=vmem_mib * MIB)


def _dot(a, b):
    return jnp.dot(a, b, preferred_element_type=F32)


def _wdot(a, w_ref):
    return _dot(a, w_ref[...].astype(BF16))


def _sigmoid(x):
    return 1.0 / (1.0 + jnp.exp(-x))


def _silu(x):
    return x * _sigmoid(x)


def _cast_kernel(w_ref, o_ref):
    o_ref[...] = w_ref[...].astype(BF16)


def _to_bf16(w, layer, rows, cols, col_start=0):
    _, k, n = w.shape
    first = col_start // cols
    return pl.pallas_call(
        _cast_kernel,
        grid=(k // rows, (n - col_start) // cols),
        in_specs=[pl.BlockSpec((None, rows, cols), lambda i, j: (layer, i, first + j))],
        out_specs=pl.BlockSpec((rows, cols), lambda i, j: (i, j)),
        out_shape=jax.ShapeDtypeStruct((k, n - col_start), BF16),
        compiler_params=_params(("arbitrary",) * 2, 48),
        name="cast_weights",
    )(w)


ROW_T = 128
N_PROMPT_TILES = M_PROMPT // ROW_T


def _rms_scale(x, g):
    ms = jnp.mean(x * x, axis=-1, keepdims=True)
    return x * lax.rsqrt(ms + EPS) * g


def _prompt_tile(i):
    return (jnp.minimum(i, N_PROMPT_TILES - 1), 0)


def _prenorm_kernel(xp_ref, xs_ref, g_ref, o_ref):
    i = pl.program_id(0)

    @pl.when(i < N_PROMPT_TILES)
    def _():
        o_ref[...] = _rms_scale(xp_ref[...], g_ref[...]).astype(BF16)

    @pl.when(i == N_PROMPT_TILES)
    def _():
        o_ref[...] = _rms_scale(xs_ref[...], g_ref[...]).astype(BF16)


def _prenorm(xp, xs, g):
    const = lambda i: (0, 0)
    return pl.pallas_call(
        _prenorm_kernel,
        grid=(N_PROMPT_TILES + 1,),
        in_specs=[pl.BlockSpec((ROW_T, D_MODEL), _prompt_tile),
                  pl.BlockSpec((ROW_T, D_MODEL), const),
                  pl.BlockSpec((1, D_MODEL), const)],
        out_specs=pl.BlockSpec((ROW_T, D_MODEL), lambda i: (i, 0)),
        out_shape=jax.ShapeDtypeStruct((M_ROWS, D_MODEL), BF16),
        compiler_params=_params(("arbitrary",), 32),
        name="prenorm",
    )(xp, xs, g.reshape(1, D_MODEL))


def _w_in_spec(l, off):
    return pl.BlockSpec((None, D_MODEL, BN), lambda i, j: (l, 0, off + j))


def _xn_spec():
    return pl.BlockSpec((BM, D_MODEL), lambda i, j: (i, 0))


def _tile_spec():
    return pl.BlockSpec((BM, BN), lambda i, j: (i, j))


def _inproj_a_kernel(xn_ref, wl_ref, wg_ref, wt_ref, glu_ref, sa_ref):
    xn = xn_ref[...]
    lin = _wdot(xn, wl_ref)
    gl = _wdot(xn, wg_ref)
    glu_ref[...] = lin * _sigmoid(gl)
    gate = _wdot(xn, wt_ref)
    sa_ref[...] = _silu(gate).astype(BF16)


def _inproj_a(xn, w_in, l):
    return pl.pallas_call(
        _inproj_a_kernel,
        grid=(N_ROW_TILES, D_A // BN),
        in_specs=[_xn_spec(), _w_in_spec(l, OFF_A_LIN), _w_in_spec(l, OFF_A_GLU),
                  _w_in_spec(l, OFF_A_GATE)],
        out_specs=[_tile_spec(), _tile_spec()],
        out_shape=[jax.ShapeDtypeStruct((M_ROWS, D_A), F32),
                   jax.ShapeDtypeStruct((M_ROWS, D_A), BF16)],
        compiler_params=_params(("arbitrary", "arbitrary"), 56),
        name="inproj_a",
    )(xn, w_in, w_in, w_in)


BN_B = 2 * KV_W
NB_Q = D_B // BN_B
NB_KV = 2 * KV_W // BN_B
NB_B = 2 * NB_Q + NB_KV


def _inproj_b_kernel(xn_ref, w_ref, q_ref, kv_ref, sb_ref):
    j = pl.program_id(1)
    acc = _wdot(xn_ref[...], w_ref)

    @pl.when(j < NB_Q)
    def _():
        q_ref[...] = (acc * (HEAD_DIM ** -0.5)).astype(BF16)

    @pl.when(jnp.logical_and(j >= NB_Q, j < NB_Q + NB_KV))
    def _():
        kv_ref[...] = acc

    @pl.when(j >= NB_Q + NB_KV)
    def _():
        sb_ref[...] = _silu(acc).astype(BF16)


def _inproj_b(xn, w_in, l):
    kv_map = lambda i, j: (i, jnp.clip(j - NB_Q, 0, NB_KV - 1))
    return pl.pallas_call(
        _inproj_b_kernel,
        grid=(N_ROW_TILES, NB_B),
        in_specs=[_xn_spec(),
                  pl.BlockSpec((None, D_MODEL, BN_B),
                               lambda i, j: (l, 0, OFF_B * BN // BN_B + j))],
        out_specs=[pl.BlockSpec((BM, BN_B), lambda i, j: (i, jnp.minimum(j, NB_Q - 1))),
                   pl.BlockSpec((BM, BN_B), kv_map),
                   pl.BlockSpec((BM, BN_B),
                                lambda i, j: (i, jnp.clip(j - NB_Q - NB_KV, 0, NB_Q - 1)))],
        out_shape=[jax.ShapeDtypeStruct((M_ROWS, D_B), BF16),
                   jax.ShapeDtypeStruct((M_ROWS, 2 * KV_W), F32),
                   jax.ShapeDtypeStruct((M_ROWS, D_B), BF16)],
        compiler_params=_params(("arbitrary", "arbitrary"), 56),
        name="inproj_b",
    )(xn, w_in)


def _inproj_c_kernel(xn_ref, wu_ref, wv_ref, wt_ref, ug_ref, cv_ref):
    xn = xn_ref[...]
    u = _wdot(xn, wu_ref)
    gate = _wdot(xn, wt_ref)
    ug_ref[...] = (u * _silu(gate)).astype(BF16)
    cv_ref[...] = _wdot(xn, wv_ref)


def _inproj_c(xn, w_in, l):
    return pl.pallas_call(
        _inproj_c_kernel,
        grid=(N_ROW_TILES, D_C // BN),
        in_specs=[_xn_spec(), _w_in_spec(l, OFF_C_U), _w_in_spec(l, OFF_C_V),
                  _w_in_spec(l, OFF_C_GATE)],
        out_specs=[_tile_spec(), _tile_spec()],
        out_shape=[jax.ShapeDtypeStruct((M_ROWS, D_C), BF16),
                   jax.ShapeDtypeStruct((M_ROWS, D_C), F32)],
        compiler_params=_params(("arbitrary", "arbitrary"), 56),
        name="inproj_c",
    )(xn, w_in, w_in, w_in)


def _ln_silu_gate(y, g, b, gate):
    mu = jnp.mean(y, axis=-1, keepdims=True)
    yc = y - mu
    var = jnp.mean(yc * yc, axis=-1, keepdims=True)
    yn = yc * lax.rsqrt(var + EPS) * g + b
    return (_silu(yn) * gate.astype(F32)).astype(BF16)


N_CONV_TILES = M_PROMPT // CONV_T
CONV_TILES_PER_SEQ = SEQ // CONV_T


def _conv_prompt_kernel(cur_ref, prev_ref, sa_ref, w_ref, cb_ref, g_ref, b_ref, out_ref,
                        ext_ref, sh_ref, y_ref):
    t = pl.program_id(0)

    @pl.when(t < N_CONV_TILES)
    def _():
        _conv_tile(t % CONV_TILES_PER_SEQ == 0, cur_ref, prev_ref, sa_ref, w_ref, cb_ref,
                   g_ref, b_ref, out_ref, ext_ref, sh_ref, y_ref)

    @pl.when(t == N_CONV_TILES)
    def _():
        out_ref[...] = jnp.zeros(out_ref.shape, out_ref.dtype)


def _conv_tile(starts_sequence, cur_ref, prev_ref, sa_ref, w_ref, cb_ref, g_ref, b_ref, out_ref,
               ext_ref, sh_ref, y_ref):
    ext_ref[0:CONV_HALO, :] = jnp.where(starts_sequence, 0.0, prev_ref[...])
    ext_ref[CONV_HALO:CONV_HALO + CONV_T, :] = cur_ref[...]
    for s in range(1, SUBLANES):
        sh_ref[s - 1] = ext_ref[s:s + CONV_SH_ROWS, :]
    first = CONV_HALO - (CONV_W - 1)
    groups = range(0, CONV_T, SUBLANES)
    for c0 in range(0, D_A, CONV_LC):
        cs = slice(c0, c0 + CONV_LC)
        bias = jnp.broadcast_to(cb_ref[:, cs], (SUBLANES, CONV_LC))
        acc = [bias for _ in groups]
        for j in range(CONV_W):
            phase = (first + j) % SUBLANES
            base = first + j - phase
            wj = jnp.broadcast_to(w_ref[j:j + 1, cs], (SUBLANES, CONV_LC))
            for gi, r0 in enumerate(groups):
                rows = slice(base + r0, base + r0 + SUBLANES)
                window = ext_ref[rows, cs] if phase == 0 else sh_ref[phase - 1, rows, cs]
                acc[gi] = acc[gi] + wj * window
        for gi, r0 in enumerate(groups):
            y_ref[r0:r0 + SUBLANES, cs] = acc[gi]
    out_ref[...] = _ln_silu_gate(y_ref[...], g_ref[...], b_ref[...], sa_ref[...])


def _conv_prompt(glu, sa, conv_w, conv_b, ln_g, ln_b):
    halo_per_tile = CONV_T // CONV_HALO
    row = lambda t: (jnp.minimum(t, N_CONV_TILES - 1), 0)
    prev = lambda t: (jnp.maximum(jnp.minimum(t, N_CONV_TILES - 1) * halo_per_tile - 1, 0), 0)
    const = lambda t: (0, 0)
    return pl.pallas_call(
        _conv_prompt_kernel,
        grid=(N_CONV_TILES + 1,),
        in_specs=[pl.BlockSpec((CONV_T, D_A), row),
                  pl.BlockSpec((CONV_HALO, D_A), prev),
                  pl.BlockSpec((CONV_T, D_A), row),
                  pl.BlockSpec((CONV_W, D_A), const),
                  pl.BlockSpec((1, D_A), const),
                  pl.BlockSpec((1, D_A), const),
                  pl.BlockSpec((1, D_A), const)],
        out_specs=pl.BlockSpec((CONV_T, D_A), lambda t: (t, 0)),
        out_shape=jax.ShapeDtypeStruct((M_ROWS, D_A), BF16),
        scratch_shapes=[pltpu.VMEM((CONV_HALO + CONV_T, D_A), F32),
                        pltpu.VMEM((SUBLANES - 1, CONV_SH_ROWS, D_A), F32),
                        pltpu.VMEM((CONV_T, D_A), F32)],
        compiler_params=_params(("arbitrary",), 40),
        name="conv_prompt",
    )(glu, glu, sa, conv_w, conv_b.reshape(1, D_A), ln_g.reshape(1, D_A),
      ln_b.reshape(1, D_A))


def _conv_sample_kernel(za_in_ref, st_ref, glu_ref, sa_ref, w_ref, cb_ref, g_ref, b_ref,
                        out_ref):
    del za_in_ref
    acc = cb_ref[...] + w_ref[CONV_W - 1:CONV_W, :] * glu_ref[0:DEC_BATCH, :]
    for j in range(CONV_W - 1):
        acc = acc + w_ref[j:j + 1, :] * st_ref[:, j, :]
    out_ref[0:DEC_BATCH, :] = _ln_silu_gate(acc, g_ref[...], b_ref[...],
                                            sa_ref[0:DEC_BATCH, :])
    out_ref[DEC_BATCH:, :] = jnp.zeros((M_SAMPLE_PAD - DEC_BATCH, D_A), BF16)


def _conv_sample(za, state, glu, sa, conv_w, conv_b, ln_g, ln_b):
    sample_tile = lambda i: (M_PROMPT // M_SAMPLE_PAD, 0)
    const = lambda i: (0, 0)
    return pl.pallas_call(
        _conv_sample_kernel,
        grid=(1,),
        in_specs=[pl.BlockSpec(memory_space=pl.ANY),
                  pl.BlockSpec((DEC_BATCH, CONV_W - 1, D_A), lambda i: (0, 0, 0)),
                  pl.BlockSpec((M_SAMPLE_PAD, D_A), sample_tile),
                  pl.BlockSpec((M_SAMPLE_PAD, D_A), sample_tile),
                  pl.BlockSpec((CONV_W, D_A), const),
                  pl.BlockSpec((1, D_A), const),
                  pl.BlockSpec((1, D_A), const),
                  pl.BlockSpec((1, D_A), const)],
        out_specs=pl.BlockSpec((M_SAMPLE_PAD, D_A), sample_tile),
        out_shape=jax.ShapeDtypeStruct((M_ROWS, D_A), BF16),
        input_output_aliases={0: 0},
        compiler_params=_params(("arbitrary",), 40),
        name="conv_sample",
    )(za, state, glu, sa, conv_w, conv_b.reshape(1, D_A), ln_g.reshape(1, D_A),
      ln_b.reshape(1, D_A))


def _softmax_sink_keys_major(s, sink):
    m = jnp.maximum(jnp.max(s, axis=0, keepdims=True), sink)
    p = jnp.exp(s - m)
    denom = jnp.sum(p, axis=0, keepdims=True) + jnp.exp(sink - m)
    return p / denom


BLOCKS_PER_SEQ = SEQ // CHUNK


def _attn_prompt_kernel(sink_ref, q_ref, kp_ref, kc_ref, vp_ref, vc_ref, sb_ref, out_ref,
                        s_ref, p_ref, vbd_ref):
    t = pl.program_id(0)

    @pl.when(t < N_CHUNKS)
    def _():
        _attn_block(t % BLOCKS_PER_SEQ, sink_ref, q_ref, kp_ref, kc_ref, vp_ref, vc_ref, sb_ref,
                    out_ref, s_ref, p_ref, vbd_ref)

    @pl.when(t == N_CHUNKS)
    def _():
        out_ref[...] = jnp.zeros(out_ref.shape, out_ref.dtype)


def _attn_block(n, sink_ref, q_ref, kp_ref, kc_ref, vp_ref, vc_ref, sb_ref, out_ref,
                s_ref, p_ref, vbd_ref):
    nk = 2 * CHUNK
    kc_i = lax.broadcasted_iota(jnp.int32, (nk, CHUNK), 0)
    qi = lax.broadcasted_iota(jnp.int32, (nk, CHUNK), 1)
    mask = jnp.logical_and(kc_i >= qi, kc_i <= qi + WINDOW)
    mask = jnp.logical_and(mask, jnp.logical_or(n > 0, kc_i >= CHUNK))
    lane = lax.broadcasted_iota(jnp.int32, (nk, LANES), 1)
    lo = lane < HEAD_DIM
    zeros_t = jnp.zeros((HEAD_DIM, nk), F32)
    for slab in range(KV_W // LANES):
        ls = slice(slab * LANES, (slab + 1) * LANES)
        k_two = jnp.concatenate([kp_ref[:, ls], kc_ref[:, ls]], axis=0)
        v_two_t = jnp.concatenate([vp_ref[:, ls], vc_ref[:, ls]], axis=0).T
        k_sw = pltpu.roll(k_two, HEAD_DIM, axis=1)
        for half in range(LANES // HEAD_DIM):
            kv = slab * (LANES // HEAD_DIM) + half
            k_lo, k_hi = (k_two, k_sw) if half == 0 else (k_sw, k_two)
            kbd = jnp.concatenate([jnp.where(lo, k_lo, 0.0), jnp.where(lo, 0.0, k_hi)],
                                  axis=0).astype(BF16)
            vt = v_two_t[half * HEAD_DIM:(half + 1) * HEAD_DIM]
            vbd_ref[kv] = jnp.concatenate(
                [jnp.concatenate([vt, zeros_t], axis=1),
                 jnp.concatenate([zeros_t, vt], axis=1)], axis=0).astype(BF16)
            for pair in range(GQ // 2):
                h0 = kv * GQ + 2 * pair
                cs = slice(h0 * HEAD_DIM, h0 * HEAD_DIM + LANES)
                s_ref[h0 // 2] = lax.dot_general(kbd, q_ref[:, cs], (((1,), (1,)), ((), ())),
                                                 preferred_element_type=F32)
    for h in range(N_HEADS):
        rows = slice((h % 2) * nk, (h % 2 + 1) * nk)
        s = jnp.where(mask, s_ref[h // 2, rows, :], -jnp.inf)
        p_ref[h // 2, rows, :] = _softmax_sink_keys_major(s, sink_ref[h]).astype(BF16)
    for kv in range(N_KV):
        for pair in range(GQ // 2):
            h0 = kv * GQ + 2 * pair
            cs = slice(h0 * HEAD_DIM, h0 * HEAD_DIM + LANES)
            o = _dot(vbd_ref[kv], p_ref[h0 // 2]).T
            out_ref[:, cs] = (o * sb_ref[:, cs].astype(F32)).astype(BF16)


def _attn_prompt(sink, q, kvb, sb):
    def cur(t):
        return jnp.minimum(t, N_CHUNKS - 1)

    def prev(t):
        c = cur(t)
        return jnp.where(c % BLOCKS_PER_SEQ == 0, c, c - 1)

    row = lambda t: (cur(t), 0)
    prev_k = lambda t: (prev(t), 0)
    prev_v = lambda t: (prev(t), 1)
    cur_v = lambda t: (cur(t), 1)
    return pl.pallas_call(
        _attn_prompt_kernel,
        grid=(N_CHUNKS + 1,),
        in_specs=[pl.BlockSpec(memory_space=pltpu.SMEM),
                  pl.BlockSpec((CHUNK, D_B), row),
                  pl.BlockSpec((CHUNK, KV_W), prev_k),
                  pl.BlockSpec((CHUNK, KV_W), row),
                  pl.BlockSpec((CHUNK, KV_W), prev_v),
                  pl.BlockSpec((CHUNK, KV_W), cur_v),
                  pl.BlockSpec((CHUNK, D_B), row)],
        out_specs=pl.BlockSpec((CHUNK, D_B), lambda t: (t, 0)),
        out_shape=jax.ShapeDtypeStruct((M_ROWS, D_B), BF16),
        scratch_shapes=[pltpu.VMEM((N_HEADS // 2, 4 * CHUNK, CHUNK), F32),
                        pltpu.VMEM((N_HEADS // 2, 4 * CHUNK, CHUNK), BF16),
                        pltpu.VMEM((N_KV, LANES, 4 * CHUNK), BF16)],
        compiler_params=_params(("arbitrary",), 32),
        name="attn_prompt",
    )(sink, q, kvb, kvb, kvb, kvb, sb)


def _attn_sample_kernel(sink_ref, q_ref, kc_ref, vc_ref, kn_ref, vn_ref, sb_ref, out_ref):
    gi = lax.broadcasted_iota(jnp.int32, (1, GQ, 1), 1)
    for kv in range(N_KV):
        q = q_ref[kv]
        qf = q.astype(F32)
        kc = kc_ref[:, :, kv, :].astype(BF16)
        vc = vc_ref[:, :, kv, :].astype(BF16)
        kn = kn_ref[kv].astype(BF16).astype(F32)
        vn = vn_ref[kv].astype(BF16).astype(F32)
        sink = jnp.zeros((1, GQ, 1), F32)
        for g in range(GQ):
            sink = jnp.where(gi == g, sink_ref[kv * GQ + g], sink)
        s_c = lax.dot_general(q, kc, (((2,), (2,)), ((0,), (0,))),
                              preferred_element_type=F32)
        s_n = jnp.sum(qf * kn[:, None, :], axis=-1, keepdims=True)
        m = jnp.maximum(jnp.maximum(jnp.max(s_c, axis=-1, keepdims=True), s_n), sink)
        p_c = jnp.exp(s_c - m)
        p_n = jnp.exp(s_n - m)
        denom = jnp.sum(p_c, axis=-1, keepdims=True) + p_n + jnp.exp(sink - m)
        p_c = (p_c / denom).astype(BF16)
        p_n = (p_n / denom).astype(BF16).astype(F32)
        o = lax.dot_general(p_c, vc, (((2,), (1,)), ((0,), (0,))),
                            preferred_element_type=F32)
        o = o + p_n * vn[:, None, :]
        out_ref[kv] = (o * sb_ref[kv].astype(F32)).astype(BF16)


def _attn_sample(sink, q_s, k_cache, v_cache, k_new, v_new, sb_s):
    return pl.pallas_call(
        _attn_sample_kernel,
        in_specs=[pl.BlockSpec(memory_space=pltpu.SMEM)]
        + [pl.BlockSpec(memory_space=pltpu.VMEM)] * 6,
        out_specs=pl.BlockSpec(memory_space=pltpu.VMEM),
        out_shape=jax.ShapeDtypeStruct((N_KV, DEC_BATCH, GQ, HEAD_DIM), BF16),
        compiler_params=pltpu.CompilerParams(vmem_limit_bytes=48 * MIB),
        name="attn_sample",
    )(sink, q_s, k_cache, v_cache, k_new, v_new, sb_s)


N_CHUNKS = M_PROMPT // CHUNK


def _sgu_kernel(cv_ref, ug_ref, w_ref, bt_ref, g_ref, b_ref, out_ref, vn_ref):
    c = pl.program_id(0)
    is_sample = c == N_CHUNKS
    x = cv_ref[...]
    mu = jnp.mean(x, axis=-1, keepdims=True)
    xc = x - mu
    var = jnp.mean(xc * xc, axis=-1, keepdims=True)
    vn = xc * lax.rsqrt(var + EPS) * g_ref[...] + b_ref[...]

    @pl.when(is_sample)
    def _():
        vn_ref[...] = vn

    vnb = vn.astype(BF16)
    ri = lax.broadcasted_iota(jnp.int32, (CHUNK, CHUNK), 0)
    ci = lax.broadcasted_iota(jnp.int32, (CHUNK, CHUNK), 1)
    tril = ri >= ci
    diag = ri == ci
    for g in range(SGU_GROUPS):
        gs = slice(g * CHUNK, (g + 1) * CHUNK)
        w = w_ref[g]
        w_first = jnp.broadcast_to(w[0:1, 0:1], (CHUNK, CHUNK))
        wm = jnp.where(is_sample, jnp.where(diag, w_first, 0.0), jnp.where(tril, w, 0.0))
        bias = bt_ref[:, g:g + 1]
        bias = jnp.where(is_sample, jnp.broadcast_to(bias[0:1, :], (CHUNK, 1)), bias)
        z = _dot(wm.astype(BF16), vnb[:, gs]) + bias
        out_ref[:, gs] = (ug_ref[:, gs].astype(F32) * z).astype(BF16)


def _sgu(cv, ug, sgu_w, sgu_bt, ln_g, ln_b):
    row = lambda c: (c, 0)
    const = lambda c: (0, 0)
    return pl.pallas_call(
        _sgu_kernel,
        grid=(N_CHUNKS + 1,),
        in_specs=[pl.BlockSpec((CHUNK, D_C), row),
                  pl.BlockSpec((CHUNK, D_C), row),
                  pl.BlockSpec((SGU_GROUPS, CHUNK, CHUNK), lambda c: (0, 0, 0)),
                  pl.BlockSpec((CHUNK, SGU_GROUPS), const),
                  pl.BlockSpec((1, D_C), const),
                  pl.BlockSpec((1, D_C), const)],
        out_specs=[pl.BlockSpec((CHUNK, D_C), row),
                   pl.BlockSpec((CHUNK, D_C), const)],
        out_shape=[jax.ShapeDtypeStruct((M_ROWS, D_C), BF16),
                   jax.ShapeDtypeStruct((CHUNK, D_C), F32)],
        compiler_params=_params(("arbitrary",), 32),
        name="sgu",
    )(cv, ug, sgu_w, sgu_bt, ln_g.reshape(1, D_C), ln_b.reshape(1, D_C))


def _merge_kernel(xn_ref, za_ref, zb_ref, zc_ref, wma_ref, wmb_ref, wmc_ref,
                  wa_ref, wb_ref, wc_ref, out_ref):
    xn = xn_ref[...]
    acc = _sigmoid(_dot(xn, wma_ref[...])) * _dot(za_ref[...], wa_ref[...])
    acc = acc + _sigmoid(_dot(xn, wmb_ref[...])) * _dot(zb_ref[...], wb_ref[...])
    acc = acc + _sigmoid(_dot(xn, wmc_ref[...])) * _dot(zc_ref[...], wc_ref[...])
    out_ref[...] = acc.astype(BF16)


def _merge(xn, za, zb, zc, w_gates, w_a, w_b, w_c):
    gate_spec = lambda g: pl.BlockSpec((D_MODEL, BN),
                                       lambda i, j: (0, g * (D_MODEL // BN) + j))
    once = pl.Buffered(1)
    z_spec = pl.BlockSpec((BM, D_A), lambda i, j: (i, 0), pipeline_mode=once)
    w_out_spec = pl.BlockSpec((D_A, BN), lambda i, j: (0, j))
    return pl.pallas_call(
        _merge_kernel,
        grid=(N_ROW_TILES, D_MODEL // BN),
        in_specs=[pl.BlockSpec((BM, D_MODEL), lambda i, j: (i, 0), pipeline_mode=once),
                  z_spec, z_spec, z_spec,
                  gate_spec(0), gate_spec(1), gate_spec(2),
                  w_out_spec, w_out_spec, w_out_spec],
        out_specs=_tile_spec(),
        out_shape=jax.ShapeDtypeStruct((M_ROWS, D_MODEL), BF16),
        compiler_params=_params(("arbitrary", "arbitrary"), 56),
        name="merge",
    )(xn, za, zb, zc, w_gates, w_gates, w_gates, w_a, w_b, w_c)


OUT_BN = 512


def _outproj_kernel(m_ref, w_ref, *cast_refs):
    n_cast = len(cast_refs) // 2
    y_ref = cast_refs[n_cast]
    y_ref[...] = _wdot(m_ref[...], w_ref)
    for src_ref, dst_ref in zip(cast_refs[:n_cast], cast_refs[n_cast + 1:]):
        dst_ref[...] = src_ref[...].astype(BF16)


GATE_COLS = 3 * D_MODEL
OUT_COL_TILES = D_MODEL // OUT_BN
GATE_CAST_TILE = (D_MODEL // N_ROW_TILES, GATE_COLS // OUT_COL_TILES)
BRANCH_CAST_TILE = (D_A // N_ROW_TILES, D_MODEL // OUT_COL_TILES)


def _outproj(merged, w_o, l, next_merge_weights=None):
    in_specs = [pl.BlockSpec((BM, D_MODEL), lambda i, j: (i, 0)),
                pl.BlockSpec((None, D_MODEL, OUT_BN), lambda i, j: (l, 0, j))]
    out_specs = [pl.BlockSpec((BM, OUT_BN), lambda i, j: (i, j))]
    out_shape = [jax.ShapeDtypeStruct((M_ROWS, D_MODEL), F32)]
    operands = [merged, w_o]
    if next_merge_weights is not None:
        w_in, w_a, w_b, w_c = next_merge_weights
        first_gate_tile = (D_IN - GATE_COLS) // GATE_CAST_TILE[1]
        in_specs.append(pl.BlockSpec((None,) + GATE_CAST_TILE,
                                     lambda i, j: (l + 1, i, first_gate_tile + j)))
        out_specs.append(pl.BlockSpec(GATE_CAST_TILE, lambda i, j: (i, j)))
        out_shape.append(jax.ShapeDtypeStruct((D_MODEL, GATE_COLS), BF16))
        for _ in range(3):
            in_specs.append(pl.BlockSpec((None,) + BRANCH_CAST_TILE,
                                         lambda i, j: (l + 1, i, j)))
            out_specs.append(pl.BlockSpec(BRANCH_CAST_TILE, lambda i, j: (i, j)))
            out_shape.append(jax.ShapeDtypeStruct((D_A, D_MODEL), BF16))
        operands += [w_in, w_a, w_b, w_c]
    return pl.pallas_call(
        _outproj_kernel,
        grid=(N_ROW_TILES, OUT_COL_TILES),
        in_specs=in_specs,
        out_specs=out_specs,
        out_shape=out_shape,
        compiler_params=_params(("arbitrary", "arbitrary"), 58),
        name="outproj",
    )(*operands)


def _postnorm_kernel(xp_ref, xs_ref, y_ref, gpost_ref, gpre_ref, xpo_ref, xso_ref, xn_ref):
    i = pl.program_id(0)

    def step(x_ref, xo_ref):
        x = x_ref[...] + _rms_scale(y_ref[...], gpost_ref[...])
        xo_ref[...] = x
        xn_ref[...] = _rms_scale(x, gpre_ref[...]).astype(BF16)

    @pl.when(i < N_PROMPT_TILES)
    def _():
        step(xp_ref, xpo_ref)

    @pl.when(i == N_PROMPT_TILES)
    def _():
        step(xs_ref, xso_ref)


def _postnorm(xp, xs, y, g_post, g_pre_next, in_place):
    row = lambda i: (i, 0)
    const = lambda i: (0, 0)
    return pl.pallas_call(
        _postnorm_kernel,
        grid=(N_PROMPT_TILES + 1,),
        in_specs=[pl.BlockSpec((ROW_T, D_MODEL), _prompt_tile),
                  pl.BlockSpec((ROW_T, D_MODEL), const),
                  pl.BlockSpec((ROW_T, D_MODEL), row),
                  pl.BlockSpec((1, D_MODEL), const),
                  pl.BlockSpec((1, D_MODEL), const)],
        out_specs=[pl.BlockSpec((ROW_T, D_MODEL), _prompt_tile),
                   pl.BlockSpec((ROW_T, D_MODEL), const),
                   pl.BlockSpec((ROW_T, D_MODEL), row)],
        out_shape=[jax.ShapeDtypeStruct((M_PROMPT, D_MODEL), F32),
                   jax.ShapeDtypeStruct((M_SAMPLE_PAD, D_MODEL), F32),
                   jax.ShapeDtypeStruct((M_ROWS, D_MODEL), BF16)],
        input_output_aliases={0: 0, 1: 1} if in_place else {},
        compiler_params=_params(("arbitrary",), 32),
        name="postnorm",
    )(xp, xs, y, g_post.reshape(1, D_MODEL), g_pre_next.reshape(1, D_MODEL))


def _heads_major(rows):
    return rows.reshape(DEC_BATCH, N_KV, GQ, HEAD_DIM).transpose(1, 0, 2, 3)


def kernel(x_prompt, x_sample, state_conv, cache_win_k, cache_win_v, w_in, conv_w, conv_b,
           conv_ln_g, conv_ln_b, attn_sink, sgu_ln_g, sgu_ln_b, sgu_w, sgu_b,
           w_a_out, w_b_out, w_c_out, w_o, g_pre, g_post):
    merge_w = [_to_bf16(w_in, 0, 512, GATE_COLS // 8, col_start=D_IN - GATE_COLS),
               _to_bf16(w_a_out, 0, 512, D_MODEL), _to_bf16(w_b_out, 0, 512, D_MODEL),
               _to_bf16(w_c_out, 0, 512, D_MODEL)]

    xp = x_prompt.reshape(M_PROMPT, D_MODEL)
    xs = jnp.pad(x_sample.reshape(DEC_BATCH, D_MODEL),
                 ((0, M_SAMPLE_PAD - DEC_BATCH), (0, 0)))
    xn = _prenorm(xp, xs, g_pre[0])

    s_rows = slice(M_PROMPT, M_PROMPT + DEC_BATCH)
    seq_end = [(b + 1) * SEQ for b in range(BATCH)]
    glu_p, kv_p, glu_s, kv_s, cv_s = [], [], [], [], []
    for l in range(DEPTH):
        glu, sa = _inproj_a(xn, w_in, l)
        q, kv, sb = _inproj_b(xn, w_in, l)
        ug, cv = _inproj_c(xn, w_in, l)

        za = _conv_prompt(glu, sa, conv_w[l], conv_b[l], conv_ln_g[l], conv_ln_b[l])
        za = _conv_sample(za, state_conv[l], glu, sa, conv_w[l], conv_b[l],
                          conv_ln_g[l], conv_ln_b[l])

        kv_new = kv[s_rows].reshape(DEC_BATCH, 2, N_KV, HEAD_DIM)
        zb = _attn_prompt(attn_sink[l], q, kv, sb)
        zb_s = _attn_sample(
            attn_sink[l], _heads_major(q[s_rows]), cache_win_k[l], cache_win_v[l],
            kv_new[:, 0].transpose(1, 0, 2), kv_new[:, 1].transpose(1, 0, 2),
            _heads_major(sb[s_rows]))
        zb_s = zb_s.transpose(1, 0, 2, 3).reshape(DEC_BATCH, D_B)
        zb_tail = jnp.pad(zb_s, ((0, M_SAMPLE_PAD - DEC_BATCH), (0, 0)))
        zb = lax.dynamic_update_slice(zb, zb_tail, (M_PROMPT, 0))

        zc, vn_s = _sgu(cv, ug, sgu_w[l], sgu_b[l].T, sgu_ln_g[l], sgu_ln_b[l])

        merged = _merge(xn, za, zb, zc, *merge_w)
        if l + 1 < DEPTH:
            y, *merge_w = _outproj(merged, w_o, l, (w_in, w_a_out, w_b_out, w_c_out))
        else:
            y, = _outproj(merged, w_o, l)
        xp, xs, xn = _postnorm(xp, xs, y, g_post[l], g_pre[(l + 1) % DEPTH], in_place=l > 0)

        glu_p.append(jnp.stack([glu[e - (CONV_W - 1):e] for e in seq_end]))
        kv_p.append(jnp.stack([kv[e - WINDOW:e] for e in seq_end]))
        glu_s.append(glu[s_rows])
        kv_s.append(kv_new)
        cv_s.append(vn_s[:DEC_BATCH])

    kv_p = jnp.stack(kv_p).reshape(DEPTH, BATCH, WINDOW, 2, N_KV, HEAD_DIM)
    kv_s = jnp.stack(kv_s)[:, :, None]
    conv_s = jnp.concatenate([state_conv[:, :, 1:], jnp.stack(glu_s)[:, :, None]], axis=2)
    k_s = jnp.concatenate([cache_win_k[:, :, 1:], kv_s[:, :, :, 0]], axis=2)
    v_s = jnp.concatenate([cache_win_v[:, :, 1:], kv_s[:, :, :, 1]], axis=2)
    y_prompt = xp.reshape(BATCH, SEQ, D_MODEL)
    y_sample = xs[:DEC_BATCH].reshape(DEC_BATCH, 1, D_MODEL)
    return (y_prompt, y_sample, jnp.stack(glu_p), kv_p[:, :, :, 0], kv_p[:, :, :, 1],
            conv_s, k_s, v_s, jnp.stack(cv_s)[:, :, None])
```
